```python
import math
import jax, jax.numpy as jnp
from jax import lax
import numpy as np

D_MODEL = 1024
BATCH = 2
SEQ = 8192
DEPTH = 2
DEC_BATCH = 128
DEC_SEQ = 4
PAST_LEN = 8192
PAGE_SIZE = 128

EPS = 1e-6
A_HEADS = 8
QK_NOPE = 64
QK_ROPE = 32
V_HEAD = D_MODEL // A_HEADS
KV_RANK = D_MODEL // 4
Q_RANK = 3 * D_MODEL // 8
ROPE_THETA = 10000.0
Q_BLOCK = 128
ATTN_SCALE = (QK_NOPE + QK_ROPE) ** -0.5
MLA_COLS = Q_RANK + KV_RANK + QK_ROPE
B_HEAD = 64
B_HEADS = D_MODEL // B_HEAD
D_DECAY = 64
D_AAA = 64
GN_EPS = 64e-5
RWKV_COLS = 3 * D_MODEL + D_DECAY + D_AAA
N_IN = MLA_COLS + RWKV_COLS + 2 * D_MODEL
D_FF = 2816
CONV_W = 3

kernel_name = 'hybrid_mla_rwkv7_convglu_adaln_step'


def rmsnorm(x, g):
    xf = x.astype(jnp.float32)
    y = xf * lax.rsqrt(jnp.mean(xf * xf, axis=-1, keepdims=True) + EPS)
    return (y * g.astype(jnp.float32)).astype(x.dtype)


def rope_tables(pos):
    inv = ROPE_THETA ** (-jnp.arange(0, QK_ROPE, 2, dtype=jnp.float32) / QK_ROPE)
    ang = pos.astype(jnp.float32)[:, None] * inv[None, :]
    return jnp.cos(ang), jnp.sin(ang)


def apply_rope(x, cos, sin):
    half = QK_ROPE // 2
    xf = x.astype(jnp.float32)
    x1, x2 = xf[..., :half], xf[..., half:]
    return jnp.concatenate([x1 * cos - x2 * sin, x1 * sin + x2 * cos], axis=-1).astype(x.dtype)


def mla_attend(q_lat, q_pe, ckv, kpe, q_pos, k_pos):
    s = jnp.einsum('bthr,bkr->bhtk', q_lat, ckv) + jnp.einsum('bthp,bkp->bhtk', q_pe, kpe)
    s = s.astype(jnp.float32) * ATTN_SCALE
    mask = k_pos[None, :] <= q_pos[:, None]
    s = jnp.where(mask[None, None], s, -jnp.inf)
    p = jax.nn.softmax(s, axis=-1).astype(ckv.dtype)
    return jnp.einsum('bhtk,bkr->bthr', p, ckv)


def prompt_attend(q_lat, q_pe, ckv, kpe):
    b, t, h, r = q_lat.shape
    nb = t // Q_BLOCK
    pos = jnp.arange(t)
    qb = q_lat.reshape(b, nb, Q_BLOCK, h, r).swapaxes(0, 1)
    pb = q_pe.reshape(b, nb, Q_BLOCK, h, QK_ROPE).swapaxes(0, 1)
    posb = pos.reshape(nb, Q_BLOCK)
    out = lax.map(lambda blk: mla_attend(blk[0], blk[1], ckv, kpe, blk[2], pos), (qb, pb, posb))
    return out.swapaxes(0, 1).reshape(b, t, h, r)


def rwkv7_scan(r, w, k, v, a, b, s0):
    def step(s, inp):
        r_t, w_t, k_t, v_t, a_t, b_t = inp
        sa = jnp.einsum('bhij,bhj->bhi', s, a_t)
        s = s * w_t[:, :, None, :] + sa[..., None] * b_t[:, :, None, :] + v_t[..., None] * k_t[:, :, None, :]
        return s, jnp.einsum('bhij,bhj->bhi', s, r_t)
    xs = tuple(jnp.swapaxes(z.astype(jnp.float32), 0, 1) for z in (r, w, k, v, a, b))
    s_fin, y = lax.scan(step, s0.astype(jnp.float32), xs)
    return jnp.swapaxes(y, 0, 1).astype(r.dtype), s_fin.astype(s0.dtype)


def mix_branches(h, p, pos, shift_prev, wkv_prev, attend):
    bsz, t, _ = h.shape
    z = h @ p['w_in']
    c_q = z[..., :Q_RANK]
    c_kv = z[..., Q_RANK:Q_RANK + KV_RANK]
    k_pe = z[..., Q_RANK + KV_RANK:MLA_COLS]
    zb = z[..., MLA_COLS:MLA_COLS + RWKV_COLS]
    g_a = z[..., MLA_COLS + RWKV_COLS:MLA_COLS + RWKV_COLS + D_MODEL]
    g_b = z[..., MLA_COLS + RWKV_COLS + D_MODEL:]

    cos, sin = rope_tables(pos)
    q = (rmsnorm(c_q, p['g_q']) @ p['w_uq']).reshape(bsz, t, A_HEADS, QK_NOPE + QK_ROPE)
    q_nope, q_pe = q[..., :QK_NOPE], q[..., QK_NOPE:]
    q_pe = apply_rope(q_pe, cos[None, :, None, :], sin[None, :, None, :])
    ckv = rmsnorm(c_kv, p['g_kv'])
    kpe = apply_rope(k_pe, cos[None], sin[None])
    q_lat = jnp.einsum('bthn,rhn->bthr', q_nope, p['w_uk'])
    o_lat = attend(q_lat, q_pe, ckv, kpe)
    o_a = jnp.einsum('bthr,rhv->bthv', o_lat, p['w_uv']).reshape(bsz, t, D_MODEL)

    shifted = jnp.concatenate([shift_prev[:, None, :], zb[:, :-1]], axis=1)
    zm = zb + (shifted - zb) * p['mu_shift']
    r = zm[..., :D_MODEL]
    zw = zm[..., D_MODEL:D_MODEL + D_DECAY]
    k = zm[..., D_MODEL + D_DECAY:2 * D_MODEL + D_DECAY]
    v = zm[..., 2 * D_MODEL + D_DECAY:3 * D_MODEL + D_DECAY]
    za = zm[..., 3 * D_MODEL + D_DECAY:]
    w_log = -jax.nn.softplus(-(p['w0'] + jnp.tanh(zw) @ p['w2'])) - 0.5
    decay = jnp.exp(-jnp.exp(w_log.astype(jnp.float32))).astype(h.dtype)
    a_c = jax.nn.sigmoid(p['a0'] + za @ p['a2'])
    heads = lambda u: u.reshape(bsz, t, B_HEADS, B_HEAD)
    kkf = heads(k * p['k_k']).astype(jnp.float32)
    kkf = kkf / jnp.maximum(jnp.sqrt(jnp.sum(kkf * kkf, axis=-1, keepdims=True)), 1e-12)
    kk = kkf.astype(h.dtype)
    k = k * (1.0 + (a_c - 1.0) * p['k_a'])
    rh, kh, vh = heads(r), heads(k), heads(v)
    y, wkv_new = rwkv7_scan(rh, heads(decay), kh, vh, -kk, kk * heads(a_c), wkv_prev)
    yf = y.astype(jnp.float32)
    mean = jnp.mean(yf, axis=-1, keepdims=True)
    var = jnp.mean(jnp.square(yf - mean), axis=-1, keepdims=True)
    yn = ((yf - mean) * lax.rsqrt(var + GN_EPS)).reshape(bsz, t, D_MODEL).astype(h.dtype)
    yn = yn * p['lnx_w'] + p['lnx_b']
    bonus = jnp.sum(rh * kh * p['r_k'], axis=-1, keepdims=True) * vh
    o_b = yn + bonus.reshape(bsz, t, D_MODEL)

    merged = jax.nn.sigmoid(g_a) * o_a + jax.nn.sigmoid(g_b) * o_b
    return merged @ p['w_out'], ckv, kpe, wkv_new, zb[:, -1]


def conv_ffn(h, p, conv_prev):
    t = h.shape[1]
    up = h @ p['w_up']
    u, val = up[..., :D_FF], up[..., D_FF:]
    upad = jnp.concatenate([conv_prev, u], axis=1)
    conv = p['conv_b'] + sum(p['conv_w'][j] * upad[:, j:j + t] for j in range(CONV_W))
    return (jax.nn.silu(conv) * val) @ p['w_down'], upad[:, -(CONV_W - 1):]


def layer(x, c, p, pos, shift_prev, wkv_prev, conv_prev, attend):
    mod = jax.nn.silu(c) @ p['w_ada'] + p['b_ada']
    sh1, sc1, g1, sh2, sc2, g2 = jnp.split(mod[:, None, :], 6, axis=-1)
    h = rmsnorm(x, p['g_mix']) * (1.0 + sc1) + sh1
    mix, ckv, kpe, wkv_new, shift_new = mix_branches(h, p, pos, shift_prev, wkv_prev, attend)
    x = x + g1 * mix
    h = rmsnorm(x, p['g_ffn']) * (1.0 + sc2) + sh2
    f, conv_new = conv_ffn(h, p, conv_prev)
    x = x + g2 * f
    return x, (ckv, kpe, wkv_new, shift_new, conv_new)


def setup_inputs(seed: int = 0) -> dict:
    key = jax.random.key(seed)
    ks = iter(jax.random.split(key, 64))

    def nrm(shape, s):
        return jax.random.normal(next(ks), shape, jnp.float32) * s

    n_pages = PAST_LEN // PAGE_SIZE
    n_used = DEC_BATCH * n_pages
    n_pool = (n_used * 5) // 4
    page_table = jax.random.permutation(next(ks), n_pool)[:n_used].reshape(DEC_BATCH, n_pages).astype(jnp.int32)
    D = D_MODEL
    return {
        'x_prompt': nrm((BATCH, SEQ, D), 1.0),
        'x_sample': nrm((DEC_BATCH, DEC_SEQ, D), 1.0),
        'c_prompt': nrm((BATCH, D), 1.0),
        'c_sample': nrm((DEC_BATCH, D), 1.0),
        'cache_ckv': nrm((DEPTH, n_pool, PAGE_SIZE, KV_RANK), 1.0),
        'cache_kpe': nrm((DEPTH, n_pool, PAGE_SIZE, QK_ROPE), 1.0),
        'state_wkv': nrm((DEPTH, DEC_BATCH, B_HEADS, B_HEAD, B_HEAD), 0.3),
        'state_shift': nrm((DEPTH, DEC_BATCH, RWKV_COLS), 1.0),
        'state_conv': nrm((DEPTH, DEC_BATCH, CONV_W - 1, D_FF), 1.0),
        'page_table': page_table,
        'w_ada': nrm((DEPTH, D, 6 * D), 0.5 * D ** -0.5),
        'b_ada': nrm((DEPTH, 6 * D), 0.01),
        'g_mix': 1.0 + nrm((DEPTH, D), 0.05),
        'g_ffn': 1.0 + nrm((DEPTH, D), 0.05),
        'w_in': nrm((DEPTH, D, N_IN), D ** -0.5),
        'g_q': 1.0 + nrm((DEPTH, Q_RANK), 0.05),
        'w_uq': nrm((DEPTH, Q_RANK, A_HEADS * (QK_NOPE + QK_ROPE)), Q_RANK ** -0.5),
        'g_kv': 1.0 + nrm((DEPTH, KV_RANK), 0.05),
        'w_uk': nrm((DEPTH, KV_RANK, A_HEADS, QK_NOPE), KV_RANK ** -0.5),
        'w_uv': nrm((DEPTH, KV_RANK, A_HEADS, V_HEAD), KV_RANK ** -0.5),
        'mu_shift': jax.random.uniform(next(ks), (DEPTH, RWKV_COLS), jnp.float32),
        'w0': nrm((DEPTH, D), 0.5),
        'w2': nrm((DEPTH, D_DECAY, D), 0.1),
        'a0': nrm((DEPTH, D), 0.1),
        'a2': nrm((DEPTH, D_AAA, D), 0.1),
        'k_k': 0.85 + nrm((DEPTH, D), 0.05),
        'k_a': 1.0 + nrm((DEPTH, D), 0.05),
        'r_k': nrm((DEPTH, B_HEADS, B_HEAD), 0.1),
        'lnx_w': 1.0 + nrm((DEPTH, D), 0.05),
        'lnx_b': nrm((DEPTH, D), 0.01),
        'w_out': nrm((DEPTH, D, D), D ** -0.5),
        'w_up': nrm((DEPTH, D, 2 * D_FF), D ** -0.5),
        'conv_w': nrm((DEPTH, CONV_W, D_FF), 0.5),
        'conv_b': nrm((DEPTH, D_FF), 0.01),
        'w_down': nrm((DEPTH, D_FF, D), D_FF ** -0.5),
        'g_final': 1.0 + nrm((D,), 0.05),
    }


def reference(x_prompt, x_sample, c_prompt, c_sample, cache_ckv, cache_kpe, state_wkv, state_shift,
              state_conv, page_table, w_ada, b_ada, g_mix, g_ffn, w_in, g_q, w_uq, g_kv, w_uk, w_uv,
              mu_shift, w0, w2, a0, a2, k_k, k_a, r_k, lnx_w, lnx_b, w_out, w_up, conv_w, conv_b,
              w_down, g_final):
    past_len = page_table.shape[1] * cache_ckv.shape[2]
    bp, tp, _ = x_prompt.shape
    bs, ts, _ = x_sample.shape
    pos_p = jnp.arange(tp)
    pos_s = past_len + jnp.arange(ts)
    k_pos_s = jnp.arange(past_len + ts)
    xp, x_s = x_prompt, x_sample
    new_p = [[] for _ in range(5)]
    new_s = [[] for _ in range(5)]
    for l in range(DEPTH):
        p = {'w_ada': w_ada[l], 'b_ada': b_ada[l], 'g_mix': g_mix[l], 'g_ffn': g_ffn[l],
             'w_in': w_in[l], 'g_q': g_q[l], 'w_uq': w_uq[l], 'g_kv': g_kv[l], 'w_uk': w_uk[l],
             'w_uv': w_uv[l], 'mu_shift': mu_shift[l], 'w0': w0[l], 'w2': w2[l], 'a0': a0[l],
             'a2': a2[l], 'k_k': k_k[l], 'k_a': k_a[l], 'r_k': r_k[l], 'lnx_w': lnx_w[l],
             'lnx_b': lnx_b[l], 'w_out': w_out[l], 'w_up': w_up[l], 'conv_w': conv_w[l],
             'conv_b': conv_b[l], 'w_down': w_down[l]}
        xp, st_p = layer(xp, c_prompt, p, pos_p,
                         jnp.zeros((bp, RWKV_COLS), xp.dtype),
                         jnp.zeros((bp, B_HEADS, B_HEAD, B_HEAD), xp.dtype),
                         jnp.zeros((bp, CONV_W - 1, D_FF), xp.dtype),
                         prompt_attend)
        ckv_past = cache_ckv[l][page_table].reshape(bs, past_len, KV_RANK)
        kpe_past = cache_kpe[l][page_table].reshape(bs, past_len, QK_ROPE)

        def sample_attend(q_lat, q_pe, ckv, kpe, ckv_past=ckv_past, kpe_past=kpe_past):
            return mla_attend(q_lat, q_pe, jnp.concatenate([ckv_past, ckv], axis=1),
                              jnp.concatenate([kpe_past, kpe], axis=1), pos_s, k_pos_s)

        x_s, st_s = layer(x_s, c_sample, p, pos_s, state_shift[l], state_wkv[l], state_conv[l],
                          sample_attend)
        for i in range(5):
            new_p[i].append(st_p[i])
            new_s[i].append(st_s[i])
    ckv_prompt, kpe_prompt, wkv_prompt, shift_prompt, conv_prompt = [jnp.stack(u, axis=0) for u in new_p]
    ckv_sample, kpe_sample, wkv_sample, shift_sample, conv_sample = [jnp.stack(u, axis=0) for u in new_s]
    y_prompt = rmsnorm(xp, g_final)
    y_sample = rmsnorm(x_s, g_final)
    return (y_prompt, y_sample, ckv_prompt, kpe_prompt, wkv_prompt, shift_prompt, conv_prompt,
            ckv_sample, kpe_sample, wkv_sample, shift_sample, conv_sample)
```

```python
import functools

import jax
import jax.numpy as jnp
from jax import lax
from jax.experimental import pallas as pl
from jax.experimental.pallas import tpu as pltpu

F32, BF16 = jnp.float32, jnp.bfloat16

EPS = 1e-6
GN_EPS = 64e-5
ROPE_THETA = 10000.0
A_HEADS = 8
QK_NOPE = 64
QK_ROPE = 32
B_HEAD = 64
ATTN_SCALE = (QK_NOPE + QK_ROPE) ** -0.5

LANE = 128
SUBLANE = 8
V7X_VMEM_LIMIT = 56 * 1024 * 1024
S_PAD = 8


def _cp(sem):
    return pltpu.CompilerParams(dimension_semantics=sem, vmem_limit_bytes=V7X_VMEM_LIMIT)


def _dot(a, b):
    return jnp.dot(a, b, preferred_element_type=F32)


def _dot_nt(a, b):
    return lax.dot_general(a, b, (((1,), (1,)), ((), ())), preferred_element_type=F32)


def _dot_tn(a, b):
    return lax.dot_general(a, b, (((0,), (0,)), ((), ())), preferred_element_type=F32)


def _split2(x):
    hi = x.astype(BF16)
    return hi, (x - hi.astype(F32)).astype(BF16)


def _split3(x):
    hi = x.astype(BF16)
    r = x - hi.astype(F32)
    mid = r.astype(BF16)
    return hi, mid, (r - mid.astype(F32)).astype(BF16)


def _dot3(a, b):
    ah, al = _split2(a)
    bh, bl = _split2(b)
    return _dot(ah, bh) + (_dot(ah, bl) + _dot(al, bh))


def _sigmoid(x):
    return 1.0 / (1.0 + jnp.exp(-x))


def _rms(x, g):
    return x * lax.rsqrt(jnp.mean(x * x, axis=-1, keepdims=True) + EPS) * g


def _mod_spec(arr, tm, rows_per_batch):
    d = arr.shape[-1]
    if arr.ndim == 3:
        bpb = rows_per_batch // tm
        return pl.BlockSpec((None, 1, d), lambda i, *_: (i // bpb, 0, 0))
    return pl.BlockSpec((tm, d), lambda i, *_: (i, 0))


def _ada_kernel(c_ref, w_ref, b_ref, o_ref):
    c = c_ref[...]
    s = (c * _sigmoid(c)).astype(BF16)
    o_ref[...] = _dot(s, w_ref[...].astype(BF16)) + b_ref[...]


def _ada(c_all, w_ada, b_ada, tn=1536):
    depth, d, n = w_ada.shape
    bc = c_all.shape[0]
    return pl.pallas_call(
        _ada_kernel,
        grid=(depth, n // tn),
        in_specs=[pl.BlockSpec((bc, d), lambda l, j: (0, 0)),
                  pl.BlockSpec((None, d, tn), lambda l, j: (l, 0, j)),
                  pl.BlockSpec((None, 1, tn), lambda l, j: (l, 0, j))],
        out_specs=pl.BlockSpec((None, bc, tn), lambda l, j: (l, 0, j)),
        out_shape=jax.ShapeDtypeStruct((depth, bc, n), F32),
        compiler_params=_cp(("arbitrary", "arbitrary")),
        name="ada",
    )(c_all, w_ada, b_ada.reshape(depth, 1, n))


def _normproj_kernel(x_ref, sc_ref, sh_ref, g_ref, w_ref, o_ref, h_ref):
    @pl.when(pl.program_id(1) == 0)
    def _():
        y = _rms(x_ref[...], g_ref[...])
        h_ref[...] = (y * (1.0 + sc_ref[...]) + sh_ref[...]).astype(BF16)

    o_ref[...] = _dot(h_ref[...], w_ref[...])


def _normproj(x, sc, sh, g, w, rows_per_batch, tm, tn):
    r, d = x.shape
    n = w.shape[1]
    return pl.pallas_call(
        _normproj_kernel,
        grid=(r // tm, n // tn),
        in_specs=[pl.BlockSpec((tm, d), lambda i, j: (i, 0)),
                  _mod_spec(sc, tm, rows_per_batch), _mod_spec(sh, tm, rows_per_batch),
                  pl.BlockSpec((1, d), lambda i, j: (0, 0)),
                  pl.BlockSpec((d, tn), lambda i, j: (0, j))],
        out_specs=pl.BlockSpec((tm, tn), lambda i, j: (i, j)),
        out_shape=jax.ShapeDtypeStruct((r, n), F32),
        scratch_shapes=[pltpu.VMEM((tm, d), BF16)],
        compiler_params=_cp(("arbitrary", "arbitrary")),
        name="normproj",
    )(x, sc, sh, g, w)


def _prep_kernel(misc_ref, gq_ref, wq_ref, wqr_ref, gkv_ref, invf_ref, *rest,
                 seq_len, pos_base, q_rank, kv_rank, sample):
    tm = misc_ref.shape[0]
    i = pl.program_id(0)
    row = lax.broadcasted_iota(jnp.int32, (tm, LANE), 0) + i * tm
    pos = (pos_base + (row & (seq_len - 1))).astype(F32)
    ang = pos * invf_ref[...]
    cosv, sinv = jnp.cos(ang), jnp.sin(ang)
    lane = lax.broadcasted_iota(jnp.int32, (tm, LANE), 1)
    is_pe = lane < QK_ROPE
    ck = jnp.where(is_pe, cosv, 0.0)
    sk = jnp.where(is_pe, sinv, 0.0)
    cq = jnp.where(is_pe, cosv, jnp.where(lane < QK_ROPE + QK_NOPE, 1.0, 0.0)) * ATTN_SCALE
    sq = sk * ATTN_SCALE

    qn = _rms(misc_ref[:, 0:q_rank], gq_ref[...]).astype(BF16)
    q = _dot(qn, wq_ref[...])
    qr = _dot(qn, wqr_ref[...])
    ckv = _rms(misc_ref[:, q_rank:q_rank + kv_rank], gkv_ref[...])
    ckv_b = ckv.astype(BF16)
    o_kpe = q_rank + kv_rank + LANE
    kpe = misc_ref[:, o_kpe:o_kpe + LANE] * ck + misc_ref[:, o_kpe + LANE:o_kpe + 2 * LANE] * sk

    if sample:
        wabs_ref, ckv_ref, kpe_ref, qlat_ref, qpe_ref = rest
        for h in range(A_HEADS):
            sl = slice(h * LANE, (h + 1) * LANE)
            qh = (q[:, sl] * cq + qr[:, sl] * sq).astype(BF16)
            qpe_ref[:, sl] = qh
            qlat_ref[:, h * kv_rank:(h + 1) * kv_rank] = _dot(qh, wabs_ref[h]).astype(BF16)
    else:
        wk_ref, wv_ref, ckv_ref, kpe_ref, q_ref, k_ref, v_ref = rest
        kn = _dot(ckv_b, wk_ref[...])
        for h in range(A_HEADS):
            sl = slice(h * LANE, (h + 1) * LANE)
            q_ref[:, sl] = (q[:, sl] * cq + qr[:, sl] * sq).astype(BF16)
            k_ref[:, sl] = (kn[:, sl] + kpe).astype(BF16)
        v_ref[...] = _dot(ckv_b, wv_ref[...]).astype(BF16)
    ckv_ref[...] = ckv
    kpe_ref[...] = kpe[:, 0:QK_ROPE]


def _prep(z, gq, wq, wqr, gkv, invf, extra, *, seq_len, pos_base, sample, tm):
    r = z.shape[0]
    q_rank, kv_rank = gq.shape[1], gkv.shape[1]
    d = wq.shape[1]
    misc_blk = z.shape[1] // d - 1
    full = lambda a: pl.BlockSpec(a.shape, lambda i: (0,) * a.ndim)
    row = lambda n: pl.BlockSpec((tm, n), lambda i: (i, 0))
    in_specs = [pl.BlockSpec((tm, d), lambda i: (i, misc_blk)), full(gq), full(wq), full(wqr),
                full(gkv), full(invf)] + [full(a) for a in extra]
    out_specs = [row(kv_rank), row(QK_ROPE)]
    out_shape = [jax.ShapeDtypeStruct((r, kv_rank), F32), jax.ShapeDtypeStruct((r, QK_ROPE), F32)]
    if sample:
        out_specs += [row(A_HEADS * kv_rank), row(d)]
        out_shape += [jax.ShapeDtypeStruct((r, A_HEADS * kv_rank), BF16), jax.ShapeDtypeStruct((r, d), BF16)]
    else:
        out_specs += [row(d)] * 3
        out_shape += [jax.ShapeDtypeStruct((r, d), BF16)] * 3
    return pl.pallas_call(
        functools.partial(_prep_kernel, seq_len=seq_len, pos_base=pos_base, q_rank=q_rank,
                          kv_rank=kv_rank, sample=sample),
        grid=(r // tm,),
        in_specs=in_specs, out_specs=out_specs, out_shape=out_shape,
        compiler_params=_cp(("arbitrary",)),
        name="mla_prep_sample" if sample else "mla_prep_prompt",
    )(z, gq, wq, wqr, gkv, invf, *extra)


def _flash_kernel(q_ref, k_ref, v_ref, o_ref, *, tq):
    qi = pl.program_id(2)
    q = q_ref[...]

    def step(k, v, carry, mask):
        m, l, acc = carry
        s = _dot_nt(q, k)
        if mask is not None:
            s = jnp.where(mask, s, -jnp.inf)
        m_new = jnp.maximum(m, jnp.max(s, axis=-1, keepdims=True))
        alpha = jnp.exp(m - m_new)
        p = jnp.exp(s - m_new)
        l = alpha * l + jnp.sum(p, axis=-1, keepdims=True)
        acc = alpha * acc + _dot(p.astype(BF16), v)
        return m_new, l, acc

    def body(j, carry):
        off = pl.multiple_of(j * tq, tq)
        return step(k_ref[pl.ds(off, tq), :], v_ref[pl.ds(off, tq), :], carry, None)

    init = (jnp.full((tq, 1), -jnp.inf, F32), jnp.zeros((tq, 1), F32), jnp.zeros((tq, LANE), F32))
    carry = lax.fori_loop(0, qi, body, init)
    off = pl.multiple_of(qi * tq, tq)
    causal = (lax.broadcasted_iota(jnp.int32, (tq, tq), 1) <= lax.broadcasted_iota(jnp.int32, (tq, tq), 0))
    _, l, acc = step(k_ref[pl.ds(off, tq), :], v_ref[pl.ds(off, tq), :], carry, causal)
    o_ref[...] = acc / l


def _flash(q, k, v, tq):
    b, t, d = q.shape
    return pl.pallas_call(
        functools.partial(_flash_kernel, tq=tq),
        grid=(b, A_HEADS, t // tq),
        in_specs=[pl.BlockSpec((None, tq, LANE), lambda b_, h, i: (b_, i, h)),
                  pl.BlockSpec((None, t, LANE), lambda b_, h, i: (b_, 0, h)),
                  pl.BlockSpec((None, t, LANE), lambda b_, h, i: (b_, 0, h))],
        out_specs=pl.BlockSpec((None, tq, LANE), lambda b_, h, i: (b_, i, h)),
        out_shape=jax.ShapeDtypeStruct((b, t, d), F32),
        compiler_params=_cp(("arbitrary", "arbitrary", "arbitrary")),
        name="flash_prompt",
    )(q, k, v)


def _sattn_kernel(pt_ref, qlat_ref, qpe_ref, knew_ref, pnew_ref, ckv_hbm, kpe_hbm, o_ref,
                  kbuf, pbuf, sem, *, n_pages, page, n_valid, ck):
    b = pl.program_id(0)
    nb = pl.num_programs(0)
    slot = b % 2

    def copies(page_of, s):
        out = []
        for p in range(n_pages):
            pg = page_of(p)
            out.append(pltpu.make_async_copy(ckv_hbm.at[pg], kbuf.at[s, pl.ds(p * page, page)], sem.at[0, s]))
            out.append(pltpu.make_async_copy(kpe_hbm.at[pg], pbuf.at[s, pl.ds(p * page, page)], sem.at[1, s]))
        return out

    @pl.when(b == 0)
    def _():
        for c in copies(lambda p: pt_ref[p], 0):
            c.start()

    @pl.when(b + 1 < nb)
    def _():
        for c in copies(lambda p: pt_ref[(b + 1) * n_pages + p], 1 - slot):
            c.start()

    for c in copies(lambda p: pt_ref[b * n_pages + p], slot):
        c.wait()

    qlat = qlat_ref[...]
    qpe = qpe_ref[...]
    nq = qlat.shape[0]

    def update(s, v, carry):
        m, l, acc = carry
        m_new = jnp.maximum(m, jnp.max(s, axis=-1, keepdims=True))
        alpha = jnp.exp(m - m_new)
        p = jnp.exp(s - m_new)
        l = alpha * l + jnp.sum(p, axis=-1, keepdims=True)
        acc = alpha * acc + _dot(p.astype(BF16), v)
        return m_new, l, acc

    def body(j, carry):
        off = pl.multiple_of(j * ck, ck)
        kc = kbuf[slot, pl.ds(off, ck), :].astype(BF16)
        pc = pbuf[slot, pl.ds(off, ck), :].astype(BF16)
        return update(_dot_nt(qlat, kc) + _dot_nt(qpe, pc), kc, carry)

    init = (jnp.full((nq, 1), -jnp.inf, F32), jnp.zeros((nq, 1), F32), jnp.zeros((nq, qlat.shape[1]), F32))
    carry = lax.fori_loop(0, n_pages * page // ck, body, init)
    kn = knew_ref[...].astype(BF16)
    pn = pnew_ref[...].astype(BF16)
    s = _dot_nt(qlat, kn) + _dot_nt(qpe, pn)
    tok = jnp.right_shift(lax.broadcasted_iota(jnp.int32, s.shape, 0), A_HEADS.bit_length() - 1)
    key = lax.broadcasted_iota(jnp.int32, s.shape, 1)
    s = jnp.where((key <= tok) & (key < n_valid), s, -jnp.inf)
    _, l, acc = update(s, kn, carry)
    o_ref[...] = acc / l


def _sattn(page_table, qlat, qpe, knew, pnew, cache_ckv, cache_kpe, n_valid):
    bs, nq, kvr = qlat.shape
    n_pages = page_table.shape[1]
    page = cache_ckv.shape[1]
    past = n_pages * page
    ck = min(1024, past)
    blk = lambda a: pl.BlockSpec((None,) + a.shape[1:], lambda b, pt: (b,) + (0,) * (a.ndim - 1))
    return pl.pallas_call(
        functools.partial(_sattn_kernel, n_pages=n_pages, page=page, n_valid=n_valid, ck=ck),
        grid_spec=pltpu.PrefetchScalarGridSpec(
            num_scalar_prefetch=1, grid=(bs,),
            in_specs=[blk(qlat), blk(qpe), blk(knew), blk(pnew),
                      pl.BlockSpec(memory_space=pl.ANY), pl.BlockSpec(memory_space=pl.ANY)],
            out_specs=pl.BlockSpec((None, nq, kvr), lambda b, pt: (b, 0, 0)),
            scratch_shapes=[pltpu.VMEM((2, past, kvr), F32), pltpu.VMEM((2, past, cache_kpe.shape[2]), F32),
                            pltpu.SemaphoreType.DMA((2, 2))]),
        out_shape=jax.ShapeDtypeStruct((bs, nq, kvr), F32),
        compiler_params=_cp(("arbitrary",)),
        name="paged_attn_sample",
    )(page_table.reshape(-1), qlat, qpe, knew, pnew, cache_ckv, cache_kpe)


def _seg_sum(x, bd):
    hi, lo = _split2(x)
    w = bd.shape[0]
    out = [_dot(hi[:, c:c + w], bd) + _dot(lo[:, c:c + w], bd) for c in range(0, x.shape[1], w)]
    return jnp.concatenate(out, axis=1)


def _scan_kernel(r_ref, k_ref, v_ref, zz_ref, inj_ref, s0_ref,
                 mu_ref, muz_ref, w0_ref, w2_ref, a0_ref, a2_ref, kk_ref, ka_ref, rk_ref,
                 lw_ref, lb_ref, tri_ref, bd_ref,
                 ob_ref, sout_ref,
                 state, carry, *, chunks_per_seq, n_valid, use_inj):
    c = pl.program_id(1)
    L, d = r_ref.shape
    npair = d // LANE
    row = lax.broadcasted_iota(jnp.int32, (L, 1), 0)
    first = c == 0

    @pl.when(first)
    def _():
        state[...] = s0_ref[...]
        carry[...] = jnp.zeros_like(carry)

    def shift(x, idx, mu):
        if use_inj:
            prev_row = inj_ref[idx:idx + 1, 0:x.shape[1]]
        else:
            prev_row = carry[idx:idx + 1, 0:x.shape[1]]
        prev = jnp.where(row == 0, prev_row, pltpu.roll(x, 1, axis=0))
        if not use_inj:
            carry[idx:idx + 1, 0:x.shape[1]] = x[L - 1:L, :]
        return x + (prev - x) * mu

    r = shift(r_ref[...], 0, mu_ref[0:1, :])
    k = shift(k_ref[...], 1, mu_ref[1:2, :])
    v = shift(v_ref[...], 2, mu_ref[2:3, :])
    zz = shift(zz_ref[...], 3, muz_ref[...])

    wl = w0_ref[...] + _dot(jnp.tanh(zz).astype(BF16), w2_ref[...])
    sp = jnp.maximum(-wl, 0.0) + jnp.log(1.0 + jnp.exp(-jnp.abs(wl)))
    logw = -jnp.exp(-sp - 0.5)
    a_c = _sigmoid(a0_ref[...] + _dot(zz.astype(BF16), a2_ref[...]))
    kkr = k * kk_ref[...]
    bd = bd_ref[...]
    kk = kkr * lax.rsqrt(jnp.maximum(_seg_sum(kkr * kkr, bd), 1e-24))
    k = k * (1.0 + (a_c - 1.0) * ka_ref[...])
    if n_valid < L:
        valid = row < n_valid
        logw = jnp.where(valid, logw, 0.0)
        kk = jnp.where(valid, kk, 0.0)
        k = jnp.where(valid, k, 0.0)
        v = jnp.where(valid, v, 0.0)
    bonus = _seg_sum(r * k * rk_ref[...], bd) * v

    l1, l2, l3 = _split3(logw)
    tri = tri_ref[...]
    cum = _dot(tri, l1) + (_dot(tri, l2) + _dot(tri, l3))
    cum_l = cum[L - 1:L, :]
    e_fwd = jnp.exp(cum)
    e_inv = jnp.exp(-cum)
    e_end = jnp.exp(cum_l - cum)
    rt = r * e_fwd
    at = -kk * jnp.exp(cum - logw)
    b = kk * a_c
    bt, kt = b * e_inv, k * e_inv
    bh, kh = b * e_end, k * e_end
    g_l = jnp.exp(cum_l)

    lane = lax.broadcasted_iota(jnp.int32, (1, LANE), 1)
    m_a = lane < B_HEAD
    tr = lax.broadcasted_iota(jnp.int32, (2 * L, 2 * L), 0)
    tc = lax.broadcasted_iota(jnp.int32, (2 * L, 2 * L), 1)
    strict, incl, eye = tr > tc, tr >= tc, (tr == tc).astype(F32)

    def stack(x):
        return jnp.concatenate([jnp.where(m_a, x, 0.0), jnp.where(m_a, 0.0, x)], axis=0).astype(BF16)

    ys = []
    for p in range(npair):
        sl = slice(p * LANE, (p + 1) * LANE)
        a_s, r_s, b_s, k_s = stack(at[:, sl]), stack(rt[:, sl]), stack(bt[:, sl]), stack(kt[:, sl])
        v_s = stack(v[:, sl])
        if (2 * L) % LANE == 0:
            g = _dot_nt(jnp.concatenate([a_s, r_s], axis=0), jnp.concatenate([b_s, k_s], axis=0))
            g_ab, g_ak, g_rb, g_rk = g[0:2 * L, 0:2 * L], g[0:2 * L, 2 * L:], g[2 * L:, 0:2 * L], g[2 * L:, 2 * L:]
        else:
            g_ab, g_ak, g_rb, g_rk = _dot_nt(a_s, b_s), _dot_nt(a_s, k_s), _dot_nt(r_s, b_s), _dot_nt(r_s, k_s)
        a_ab = jnp.where(strict, g_ab, 0.0)
        a_ak = jnp.where(strict, g_ak, 0.0).astype(BF16)
        a_rb = jnp.where(incl, g_rb, 0.0).astype(BF16)
        a_rk = jnp.where(incl, g_rk, 0.0).astype(BF16)
        t_inv = eye + a_ab
        pw = a_ab
        n = 2
        while n < L:
            pw = _dot3(pw, pw)
            t_inv = t_inv + _dot3(pw, t_inv)
            n *= 2
        s_b = state[p].astype(BF16)
        ar_s = _dot_nt(jnp.concatenate([a_s, r_s], axis=0), s_b)
        av = _dot(jnp.concatenate([a_ak, a_rk], axis=0), v_s)
        u = _dot3(t_inv, ar_s[0:2 * L] + av[0:2 * L])
        u_b = u.astype(BF16)
        y_s = ar_s[2 * L:] + av[2 * L:] + _dot(a_rb, u_b)
        ys.append(y_s[0:L] + y_s[L:])
        state[p] = state[p] * g_l[:, sl] + _dot_tn(
            jnp.concatenate([u_b, v_s], axis=0),
            jnp.concatenate([stack(bh[:, sl]), stack(kh[:, sl])], axis=0))
    y = jnp.concatenate(ys, axis=1)

    bdm = bd * (1.0 / B_HEAD)
    mean = _seg_sum(y, bdm)
    yc = y - mean
    var = _seg_sum(yc * yc, bdm)
    ob_ref[...] = yc * lax.rsqrt(var + GN_EPS) * lw_ref[...] + lb_ref[...] + bonus

    @pl.when(c == chunks_per_seq - 1)
    def _():
        sout_ref[...] = state[...]


def _scan(z, inj, s0, wts, *, n_seq, seq_len, chunk, n_valid):
    r = z.shape[0]
    d = wts[0].shape[1]
    npair = d // LANE
    cps = seq_len // chunk
    zz_blk = (5 * d + 640) // LANE
    use_inj = inj is not None
    if inj is None:
        inj = jnp.zeros((n_seq, SUBLANE, d), F32)
    tri = (jnp.arange(chunk)[:, None] >= jnp.arange(chunk)[None, :]).astype(BF16)
    bd = (jnp.arange(2 * LANE)[:, None] // B_HEAD == jnp.arange(2 * LANE)[None, :] // B_HEAD).astype(BF16)
    full = lambda a: pl.BlockSpec(a.shape, lambda b, c: (0,) * a.ndim)
    colblk = lambda j: pl.BlockSpec((chunk, d), lambda b, c: (b * cps + c, j))
    consts = list(wts) + [tri, bd]
    return pl.pallas_call(
        functools.partial(_scan_kernel, chunks_per_seq=cps, n_valid=n_valid, use_inj=use_inj),
        grid=(n_seq, cps),
        in_specs=[colblk(0), colblk(1), colblk(2),
                  pl.BlockSpec((chunk, LANE), lambda b, c: (b * cps + c, zz_blk)),
                  pl.BlockSpec((None, SUBLANE, d), lambda b, c: (b, 0, 0)),
                  pl.BlockSpec((None, npair, LANE, LANE), lambda b, c: (b, 0, 0, 0))]
                 + [full(a) for a in consts],
        out_specs=[pl.BlockSpec((chunk, d), lambda b, c: (b * cps + c, 0)),
                   pl.BlockSpec((None, npair, LANE, LANE), lambda b, c: (b, 0, 0, 0))],
        out_shape=[jax.ShapeDtypeStruct((r, d), F32),
                   jax.ShapeDtypeStruct((n_seq, npair, LANE, LANE), F32)],
        scratch_shapes=[pltpu.VMEM((npair, LANE, LANE), F32), pltpu.VMEM((SUBLANE, d), F32)],
        compiler_params=_cp(("arbitrary", "arbitrary")),
        name="rwkv_scan",
    )(z, z, z, z, inj, s0, *consts)


def _merge_kernel(oa_ref, ob_ref, ga_ref, gb_ref, x_ref, g1_ref, wo_ref, *rest, sample):
    if sample:
        wuv_ref, o_ref = rest
        kvr = wuv_ref.shape[1]
        oa = jnp.concatenate(
            [_dot(oa_ref[:, h * kvr:(h + 1) * kvr].astype(BF16), wuv_ref[h]) for h in range(A_HEADS)], axis=1)
    else:
        (o_ref,) = rest
        oa = oa_ref[...]
    merged = _sigmoid(ga_ref[...]) * oa + _sigmoid(gb_ref[...]) * ob_ref[...]
    o_ref[...] = x_ref[...] + g1_ref[...] * _dot(merged.astype(BF16), wo_ref[...])


def _merge(oa, ob, z, x, g1, wo, wuv, rows_per_batch, tm):
    r, d = x.shape
    sample = wuv is not None
    full = lambda a: pl.BlockSpec(a.shape, lambda i: (0,) * a.ndim)
    row = lambda n: pl.BlockSpec((tm, n), lambda i: (i, 0))
    in_specs = [row(oa.shape[1]), row(d), pl.BlockSpec((tm, d), lambda i: (i, 3)),
                pl.BlockSpec((tm, d), lambda i: (i, 4)), row(d), _mod_spec(g1, tm, rows_per_batch), full(wo)]
    args = [oa, ob, z, z, x, g1, wo]
    if sample:
        in_specs.append(full(wuv))
        args.append(wuv)
    return pl.pallas_call(
        functools.partial(_merge_kernel, sample=sample),
        grid=(r // tm,),
        in_specs=in_specs, out_specs=row(d),
        out_shape=jax.ShapeDtypeStruct((r, d), F32),
        compiler_params=_cp(("arbitrary",)),
        name="merge_out",
    )(*args)


def _ffn_kernel(x_ref, sc_ref, sh_ref, g2_ref, gn_ref, wu_ref, wv_ref, cw_ref, cb_ref, wd_ref, *rest,
                seq_len, sample, final):
    rest = list(rest)
    if sample:
        inj1_ref, inj2_ref = rest[:2]
        rest = rest[2:]
    if final:
        gf_ref = rest.pop(0)
    x2_ref, u_ref = rest[:2]
    rest = rest[2:]
    if final:
        y_ref = rest.pop(0)
    h_ref, acc_ref, carry_ref = rest
    i, j = pl.program_id(0), pl.program_id(1)
    tm = x_ref.shape[0]

    @pl.when(j == 0)
    def _():
        y = _rms(x_ref[...], gn_ref[...])
        h_ref[...] = (y * (1.0 + sc_ref[...]) + sh_ref[...]).astype(BF16)
        acc_ref[...] = jnp.zeros_like(acc_ref)

    h = h_ref[...]
    u = _dot(h, wu_ref[...])
    val = _dot(h, wv_ref[...])
    row = lax.broadcasted_iota(jnp.int32, (tm, 1), 0)
    r1, r2 = pltpu.roll(u, 1, axis=0), pltpu.roll(u, 2, axis=0)
    if sample:
        t = row & (seq_len - 1)
        u1 = jnp.where(t == 0, inj1_ref[...], r1)
        u2 = jnp.where(t < 2, inj2_ref[...], r2)
        u_ref[...] = u
    else:
        @pl.when(((i * tm) & (seq_len - 1)) == 0)
        def _():
            carry_ref[j] = jnp.zeros(carry_ref.shape[1:], F32)

        prev = carry_ref[j]
        u1 = jnp.where(row == 0, prev[7:8, :], r1)
        u2 = jnp.where(row == 0, prev[6:7, :], jnp.where(row == 1, prev[7:8, :], r2))
        carry_ref[j] = u[tm - SUBLANE:tm, :]
        u_ref[...] = u[tm - SUBLANE:tm, :]
    conv = cb_ref[...] + cw_ref[0:1, :] * u2 + cw_ref[1:2, :] * u1 + cw_ref[2:3, :] * u
    act = (conv * _sigmoid(conv) * val).astype(BF16)
    acc_ref[...] += _dot(act, wd_ref[...])

    @pl.when(j == pl.num_programs(1) - 1)
    def _():
        x2 = x_ref[...] + g2_ref[...] * acc_ref[...]
        x2_ref[...] = x2
        if final:
            y_ref[...] = _rms(x2, gf_ref[...])


def _ffn(x, sc, sh, g2, gn, wu, wv, cw, cb, wd, inj, gf, *, rows_per_batch, seq_len, tm, tf):
    r, d = x.shape
    f = wu.shape[1]
    sample, final = inj is not None, gf is not None
    nf = f // tf
    cst = lambda a: pl.BlockSpec(a.shape, lambda i, j: (0,) * a.ndim)
    in_specs = [pl.BlockSpec((tm, d), lambda i, j: (i, 0)),
                _mod_spec(sc, tm, rows_per_batch), _mod_spec(sh, tm, rows_per_batch),
                _mod_spec(g2, tm, rows_per_batch), cst(gn),
                pl.BlockSpec((d, tf), lambda i, j: (0, j)), pl.BlockSpec((d, tf), lambda i, j: (0, j)),
                pl.BlockSpec((3, tf), lambda i, j: (0, j)), pl.BlockSpec((1, tf), lambda i, j: (0, j)),
                pl.BlockSpec((tf, d), lambda i, j: (j, 0))]
    args = [x, sc, sh, g2, gn, wu, wv, cw, cb, wd]
    if sample:
        in_specs += [pl.BlockSpec((tm, tf), lambda i, j: (i, j))] * 2
        args += list(inj)
    if final:
        in_specs.append(cst(gf))
        args.append(gf)
    out_specs = [pl.BlockSpec((tm, d), lambda i, j: (i, 0))]
    out_shape = [jax.ShapeDtypeStruct((r, d), F32)]
    if sample:
        out_specs.append(pl.BlockSpec((tm, tf), lambda i, j: (i, j)))
        out_shape.append(jax.ShapeDtypeStruct((r, f), F32))
    else:
        out_specs.append(pl.BlockSpec((None, SUBLANE, tf), lambda i, j: (i, 0, j)))
        out_shape.append(jax.ShapeDtypeStruct((r // tm, SUBLANE, f), F32))
    if final:
        out_specs.append(pl.BlockSpec((tm, d), lambda i, j: (i, 0)))
        out_shape.append(jax.ShapeDtypeStruct((r, d), F32))
    return pl.pallas_call(
        functools.partial(_ffn_kernel, seq_len=seq_len, sample=sample, final=final),
        grid=(r // tm, nf),
        in_specs=in_specs, out_specs=out_specs, out_shape=out_shape,
        scratch_shapes=[pltpu.VMEM((tm, d), BF16), pltpu.VMEM((tm, d), F32),
                        pltpu.VMEM((nf, SUBLANE, tf), F32)],
        compiler_params=_cp(("arbitrary", "arbitrary")),
        name="conv_ffn",
    )(*args)


def _pad_cols(a, n):
    return jnp.pad(a, ((0, 0), (0, n - a.shape[1])))


def _rot(a):
    half = a.shape[-1] // 2
    return jnp.concatenate([-a[..., half:], a[..., :half]], axis=-1)


def _layer_weights(l, w_in, mu_shift, w_uq, w_uk, w_uv, w2, a2, d, q_rank, kv_rank):
    dd = B_HEAD
    o_zb = q_rank + kv_rank + QK_ROPE
    o_r, o_zw, o_k, o_v, o_za = 0, d, d + dd, 2 * d + dd, 3 * d + dd
    n_zb = 3 * d + 2 * dd
    wi = w_in[l]
    zb = wi[:, o_zb:o_zb + n_zb]
    kpe_w = wi[:, q_rank + kv_rank:o_zb]
    misc = jnp.concatenate([wi[:, :q_rank + kv_rank], zb[:, o_zw:o_zw + dd], zb[:, o_za:o_za + dd],
                            _pad_cols(kpe_w, LANE), _pad_cols(_rot(kpe_w), LANE)], axis=1)
    w_z = jnp.concatenate([zb[:, o_r:o_r + d], zb[:, o_k:o_k + d], zb[:, o_v:o_v + d],
                           wi[:, o_zb + n_zb:o_zb + n_zb + 2 * d], misc], axis=1).astype(BF16)
    mu = mu_shift[l]
    mu_rkv = jnp.stack([mu[o_r:o_r + d], mu[o_k:o_k + d], mu[o_v:o_v + d]])
    mu_zz = jnp.concatenate([mu[o_zw:o_zw + dd], mu[o_za:o_za + dd]])[None]
    wq3 = w_uq[l].reshape(q_rank, A_HEADS, QK_NOPE + QK_ROPE)
    nope, rope = wq3[..., :QK_NOPE], wq3[..., QK_NOPE:]
    zpad = jnp.zeros((q_rank, A_HEADS, LANE - QK_NOPE - QK_ROPE), F32)
    wq = jnp.concatenate([rope, nope, zpad], axis=-1).reshape(q_rank, A_HEADS * LANE).astype(BF16)
    wqr = jnp.concatenate([_rot(rope), jnp.zeros_like(nope), zpad], axis=-1).reshape(q_rank, A_HEADS * LANE).astype(BF16)
    zk = jnp.zeros((kv_rank, A_HEADS, QK_ROPE), F32)
    wk = jnp.concatenate([zk, w_uk[l], zk], axis=-1).reshape(kv_rank, A_HEADS * LANE).astype(BF16)
    wv = w_uv[l].reshape(kv_rank, -1).astype(BF16)
    ukt = jnp.transpose(w_uk[l], (1, 2, 0))
    zr = jnp.zeros((A_HEADS, QK_ROPE, kv_rank), F32)
    wabs = jnp.concatenate([zr, ukt, zr], axis=1).astype(BF16)
    wuv_h = jnp.transpose(w_uv[l], (1, 0, 2)).astype(BF16)
    zrow = jnp.zeros((dd, d), F32)
    w2p = jnp.concatenate([w2[l], zrow], axis=0).astype(BF16)
    a2p = jnp.concatenate([zrow, a2[l]], axis=0).astype(BF16)
    return dict(w_z=w_z, mu_rkv=mu_rkv, mu_zz=mu_zz, wq=wq, wqr=wqr, wk=wk, wv=wv, wabs=wabs,
                wuv_h=wuv_h, w2p=w2p, a2p=a2p)


def _unperm_shift(zrow, d):
    dd = B_HEAD
    m = 5 * d
    zw, za = zrow[:, m + 640:m + 640 + dd], zrow[:, m + 640 + dd:m + 640 + 2 * dd]
    return jnp.concatenate([zrow[:, 0:d], zw, zrow[:, d:2 * d], zrow[:, 2 * d:3 * d], za], axis=1)


def _shift_inj(state_shift_l, d):
    dd = B_HEAD
    s = state_shift_l
    r, zw, k, v, za = s[:, 0:d], s[:, d:d + dd], s[:, d + dd:2 * d + dd], s[:, 2 * d + dd:3 * d + dd], s[:, 3 * d + dd:]
    zz = jnp.concatenate([zw, za, jnp.zeros((s.shape[0], d - 2 * dd), F32)], axis=1)
    rows = jnp.stack([r, k, v, zz], axis=1)
    return jnp.concatenate([rows, jnp.zeros((s.shape[0], SUBLANE - 4, d), F32)], axis=1)


def _pair_state(s):
    n, h = s.shape[:2]
    s = s.reshape(n, h // 2, 2, B_HEAD, B_HEAD)
    z = jnp.zeros_like(s[:, :, 0])
    top = jnp.concatenate([s[:, :, 0], z], axis=-1)
    bot = jnp.concatenate([z, s[:, :, 1]], axis=-1)
    return jnp.concatenate([top, bot], axis=-2)


def _unpair_state(sp):
    n, hp = sp.shape[:2]
    a = sp[:, :, :B_HEAD, :B_HEAD]
    b = sp[:, :, B_HEAD:, B_HEAD:]
    return jnp.stack([a, b], axis=2).reshape(n, 2 * hp, B_HEAD, B_HEAD)


def kernel(x_prompt, x_sample, c_prompt, c_sample, cache_ckv, cache_kpe, state_wkv, state_shift, state_conv, page_table, w_ada, b_ada, g_mix, g_ffn, w_in, g_q, w_uq, g_kv, w_uk, w_uv, mu_shift, w0, w2, a0, a2, k_k, k_a, r_k, lnx_w, lnx_b, w_out, w_up, conv_w, conv_b, w_down, g_final):
    bp, tp, d = x_prompt.shape
    bs, ts, _ = x_sample.shape
    depth = w_in.shape[0]
    q_rank, kv_rank = g_q.shape[1], g_kv.shape[1]
    d_ff = w_down.shape[1]
    n_pages, page = page_table.shape[1], cache_ckv.shape[2]
    past_len = n_pages * page
    heads_b = d // B_HEAD
    rp, rs = bp * tp, bs * S_PAD

    tm_p = min(512, tp)
    tm_s = min(256, rs)
    tq = min(512, tp)
    chunk_p = min(64, tp)
    tf = d_ff // 2

    mod = _ada(jnp.concatenate([c_prompt, c_sample], axis=0), w_ada, b_ada)
    invf = ROPE_THETA ** (-jnp.arange(0, QK_ROPE, 2, dtype=F32) / QK_ROPE)
    invf = jnp.tile(invf, LANE // invf.shape[0])[None]

    xp = x_prompt.reshape(rp, d)
    xs = jnp.pad(x_sample, ((0, 0), (0, S_PAD - ts), (0, 0))).reshape(rs, d)
    new_p = [[] for _ in range(5)]
    new_s = [[] for _ in range(5)]
    yp = ys = None
    for l in range(depth):
        w = _layer_weights(l, w_in, mu_shift, w_uq, w_uk, w_uv, w2, a2, d, q_rank, kv_rank)
        mods_p = [m[:, None, :] for m in jnp.split(mod[l, :bp], 6, axis=-1)]
        mods_s = [jnp.repeat(m, S_PAD, axis=0) for m in jnp.split(mod[l, bp:], 6, axis=-1)]
        row1 = lambda a: a[l][None]
        scan_w = [w['mu_rkv'], w['mu_zz'], row1(w0), w['w2p'], row1(a0), w['a2p'], row1(k_k), row1(k_a),
                  r_k[l].reshape(1, d), row1(lnx_w), row1(lnx_b)]
        wu, wv_up = w_up[l][:, :d_ff].astype(BF16), w_up[l][:, d_ff:].astype(BF16)
        wd, wo = w_down[l].astype(BF16), w_out[l].astype(BF16)
        gf = g_final[None] if l == depth - 1 else None

        sh1, sc1, g1, sh2, sc2, g2 = mods_p
        z = _normproj(xp, sc1, sh1, row1(g_mix), w['w_z'], tp, tm_p, 1536)
        ckv, kpe, q, k, v = _prep(z, row1(g_q), w['wq'], w['wqr'], row1(g_kv), invf, [w['wk'], w['wv']],
                                  seq_len=tp, pos_base=0, sample=False, tm=tm_p)
        oa = _flash(q.reshape(bp, tp, d), k.reshape(bp, tp, d), v.reshape(bp, tp, d), tq).reshape(rp, d)
        ob, sp = _scan(z, None, jnp.zeros((bp, heads_b // 2, LANE, LANE), F32), scan_w,
                       n_seq=bp, seq_len=tp, chunk=chunk_p, n_valid=chunk_p)
        x1 = _merge(oa, ob, z, xp, g1, wo, None, tp, tm_p)
        outs = _ffn(x1, sc2, sh2, g2, row1(g_ffn), wu, wv_up, conv_w[l], row1(conv_b), wd, None, gf,
                    rows_per_batch=tp, seq_len=tp, tm=tm_p, tf=tf)
        xp, utail = outs[0], outs[1]
        if gf is not None:
            yp = outs[2]
        zlast = z.reshape(bp, tp, -1)[:, tp - 1]
        new_p[0].append(ckv.reshape(bp, tp, kv_rank))
        new_p[1].append(kpe.reshape(bp, tp, QK_ROPE))
        new_p[2].append(_unpair_state(sp))
        new_p[3].append(_unperm_shift(zlast, d))
        new_p[4].append(utail.reshape(bp, tp // tm_p, SUBLANE, d_ff)[:, -1, SUBLANE - 2:])

        sh1, sc1, g1, sh2, sc2, g2 = mods_s
        z = _normproj(xs, sc1, sh1, row1(g_mix), w['w_z'], rs, tm_s, 1536)
        ckv, kpe, qlat, qpe = _prep(z, row1(g_q), w['wq'], w['wqr'], row1(g_kv), invf, [w['wabs']],
                                    seq_len=S_PAD, pos_base=past_len, sample=True, tm=tm_s)
        qlat = qlat.reshape(bs, S_PAD, A_HEADS, kv_rank)[:, :ts].reshape(bs, ts * A_HEADS, kv_rank)
        qpe = qpe.reshape(bs, S_PAD, A_HEADS, LANE)[:, :ts, :, :QK_ROPE].reshape(bs, ts * A_HEADS, QK_ROPE)
        ckv3, kpe3 = ckv.reshape(bs, S_PAD, kv_rank), kpe.reshape(bs, S_PAD, QK_ROPE)
        olat = _sattn(page_table, qlat, qpe, ckv3, kpe3, cache_ckv[l], cache_kpe[l], ts)
        olat = jnp.pad(olat.reshape(bs, ts, A_HEADS * kv_rank), ((0, 0), (0, S_PAD - ts), (0, 0)))
        ob, sp = _scan(z, _shift_inj(state_shift[l], d), _pair_state(state_wkv[l]), scan_w,
                       n_seq=bs, seq_len=S_PAD, chunk=S_PAD, n_valid=ts)
        x1 = _merge(olat.reshape(rs, -1), ob, z, xs, g1, wo, w['wuv_h'], rs, tm_s)
        cp = state_conv[l]
        zc = jnp.zeros((bs, S_PAD, d_ff), F32)
        inj1 = zc.at[:, 0].set(cp[:, 1]).reshape(rs, d_ff)
        inj2 = zc.at[:, 0].set(cp[:, 0]).at[:, 1].set(cp[:, 1]).reshape(rs, d_ff)
        outs = _ffn(x1, sc2, sh2, g2, row1(g_ffn), wu, wv_up, conv_w[l], row1(conv_b), wd, (inj1, inj2), gf,
                    rows_per_batch=rs, seq_len=S_PAD, tm=tm_s, tf=tf)
        xs, u_full = outs[0], outs[1]
        if gf is not None:
            ys = outs[2]
        new_s[0].append(ckv3[:, :ts])
        new_s[1].append(kpe3[:, :ts])
        new_s[2].append(_unpair_state(sp))
        new_s[3].append(_unperm_shift(z.reshape(bs, S_PAD, -1)[:, ts - 1], d))
        new_s[4].append(u_full.reshape(bs, S_PAD, d_ff)[:, ts - 2:ts])

    stack = lambda u: jnp.stack(u, axis=0)
    return (yp.reshape(bp, tp, d), ys.reshape(bs, S_PAD, d)[:, :ts],
            stack(new_p[0]), stack(new_p[1]), stack(new_p[2]), stack(new_p[3]), stack(new_p[4]),
            stack(new_s[0]), stack(new_s[1]), stack(new_s[2]), stack(new_s[3]), stack(new_s[4]))
```

```python
import functools

import jax
import jax.numpy as jnp
from jax import lax
from jax.experimental import pallas as pl
from jax.experimental.pallas import tpu as pltpu

F32, BF16 = jnp.float32, jnp.bfloat16

EPS = 1e-6
GN_EPS = 64e-5
ROPE_THETA = 10000.0
A_HEADS = 8
QK_NOPE = 64
QK_ROPE = 32
B_HEAD = 64
ATTN_SCALE = (QK_NOPE + QK_ROPE) ** -0.5
LOG2E = 1.4426950408889634

LANE = 128
SUBLANE = 8
V7X_VMEM_LIMIT = 56 * 1024 * 1024
S_PAD = 8


def _cp(sem):
    return pltpu.CompilerParams(dimension_semantics=sem, vmem_limit_bytes=V7X_VMEM_LIMIT)


def _dot(a, b):
    return jnp.dot(a, b, preferred_element_type=F32)


def _dot_nt(a, b):
    return lax.dot_general(a, b, (((1,), (1,)), ((), ())), preferred_element_type=F32)


def _dot_tn(a, b):
    return lax.dot_general(a, b, (((0,), (0,)), ((), ())), preferred_element_type=F32)


def _split2(x):
    hi = x.astype(BF16)
    return hi, (x - hi.astype(F32)).astype(BF16)


def _split3(x):
    hi = x.astype(BF16)
    r = x - hi.astype(F32)
    mid = r.astype(BF16)
    return hi, mid, (r - mid.astype(F32)).astype(BF16)


def _dot3(a, b):
    ah, al = _split2(a)
    bh, bl = _split2(b)
    return _dot(ah, bh) + (_dot(ah, bl) + _dot(al, bh))


def _sigmoid(x):
    return 1.0 / (1.0 + jnp.exp(-x))


def _rms(x, g):
    return x * lax.rsqrt(jnp.mean(x * x, axis=-1, keepdims=True) + EPS) * g


def _mod_spec(arr, tm, rows_per_batch):
    d = arr.shape[-1]
    if arr.ndim == 3:
        bpb = rows_per_batch // tm
        return pl.BlockSpec((None, 1, d), lambda i, *_: (i // bpb, 0, 0))
    return pl.BlockSpec((tm, d), lambda i, *_: (i, 0))


def _ada_kernel(c_ref, w_ref, b_ref, o_ref):
    c = c_ref[...]
    s = (c * _sigmoid(c)).astype(BF16)
    o_ref[...] = _dot(s, w_ref[...].astype(BF16)) + b_ref[...]


def _ada(c_all, w_ada, b_ada, tn=1536):
    depth, d, n = w_ada.shape
    bc = c_all.shape[0]
    return pl.pallas_call(
        _ada_kernel,
        grid=(depth, n // tn),
        in_specs=[pl.BlockSpec((bc, d), lambda l, j: (0, 0)),
                  pl.BlockSpec((None, d, tn), lambda l, j: (l, 0, j)),
                  pl.BlockSpec((None, 1, tn), lambda l, j: (l, 0, j))],
        out_specs=pl.BlockSpec((None, bc, tn), lambda l, j: (l, 0, j)),
        out_shape=jax.ShapeDtypeStruct((depth, bc, n), F32),
        compiler_params=_cp(("arbitrary", "arbitrary")),
        name="ada",
    )(c_all, w_ada, b_ada.reshape(depth, 1, n))


def _normproj_kernel(x_ref, sc_ref, sh_ref, g_ref, w_ref, o_ref, h_ref):
    @pl.when(pl.program_id(1) == 0)
    def _():
        y = _rms(x_ref[...], g_ref[...])
        h_ref[...] = (y * (1.0 + sc_ref[...]) + sh_ref[...]).astype(BF16)

    o_ref[...] = _dot(h_ref[...], w_ref[...])


def _normproj(x, sc, sh, g, w, rows_per_batch, tm, tn):
    r, d = x.shape
    n = w.shape[1]
    return pl.pallas_call(
        _normproj_kernel,
        grid=(r // tm, n // tn),
        in_specs=[pl.BlockSpec((tm, d), lambda i, j: (i, 0)),
                  _mod_spec(sc, tm, rows_per_batch), _mod_spec(sh, tm, rows_per_batch),
                  pl.BlockSpec((1, d), lambda i, j: (0, 0)),
                  pl.BlockSpec((d, tn), lambda i, j: (0, j))],
        out_specs=pl.BlockSpec((tm, tn), lambda i, j: (i, j)),
        out_shape=jax.ShapeDtypeStruct((r, n), F32),
        scratch_shapes=[pltpu.VMEM((tm, d), BF16)],
        compiler_params=_cp(("arbitrary", "arbitrary")),
        name="normproj",
    )(x, sc, sh, g, w)


def _prep_kernel(misc_ref, gq_ref, wq_ref, wqr_ref, gkv_ref, invf_ref, *rest,
                 seq_len, pos_base, q_rank, kv_rank, sample):
    tm = misc_ref.shape[0]
    i = pl.program_id(0)
    row = lax.broadcasted_iota(jnp.int32, (tm, LANE), 0) + i * tm
    pos = (pos_base + (row & (seq_len - 1))).astype(F32)
    ang = pos * invf_ref[...]
    cosv, sinv = jnp.cos(ang), jnp.sin(ang)
    lane = lax.broadcasted_iota(jnp.int32, (tm, LANE), 1)
    is_pe = lane < QK_ROPE
    ck = jnp.where(is_pe, cosv, 0.0)
    sk = jnp.where(is_pe, sinv, 0.0)
    qscale = ATTN_SCALE if sample else ATTN_SCALE * LOG2E
    cq = jnp.where(is_pe, cosv, jnp.where(lane < QK_ROPE + QK_NOPE, 1.0, 0.0)) * qscale
    sq = sk * qscale

    qn = _rms(misc_ref[:, 0:q_rank], gq_ref[...]).astype(BF16)
    q = _dot(qn, wq_ref[...])
    qr = _dot(qn, wqr_ref[...])
    ckv = _rms(misc_ref[:, q_rank:q_rank + kv_rank], gkv_ref[...])
    ckv_b = ckv.astype(BF16)
    o_kpe = q_rank + kv_rank + LANE
    kpe = misc_ref[:, o_kpe:o_kpe + LANE] * ck + misc_ref[:, o_kpe + LANE:o_kpe + 2 * LANE] * sk

    if sample:
        wabs_ref, ckv_ref, kpe_ref, qlat_ref, qpe_ref = rest
        for h in range(A_HEADS):
            sl = slice(h * LANE, (h + 1) * LANE)
            qh = (q[:, sl] * cq + qr[:, sl] * sq).astype(BF16)
            qpe_ref[:, sl] = qh
            qlat_ref[:, h * kv_rank:(h + 1) * kv_rank] = _dot(qh, wabs_ref[h]).astype(BF16)
    else:
        wk_ref, wvt_ref, ckv_ref, kpe_ref, q_ref, k_ref, vt_ref = rest
        kn = _dot(ckv_b, wk_ref[...])
        for h in range(A_HEADS):
            sl = slice(h * LANE, (h + 1) * LANE)
            q_ref[:, sl] = (q[:, sl] * cq + qr[:, sl] * sq).astype(BF16)
            k_ref[:, sl] = (kn[:, sl] + kpe).astype(BF16)
        vt = _dot_nt(wvt_ref[...], ckv_b)
        kc = vt_ref.shape[2]
        for c in range(tm // kc):
            vt_ref[c] = vt[:, c * kc:(c + 1) * kc].astype(BF16)
    ckv_ref[...] = ckv
    kpe_ref[...] = kpe[:, 0:QK_ROPE]


def _prep(z, gq, wq, wqr, gkv, invf, extra, *, seq_len, pos_base, sample, tm, kv_chunk=None):
    r = z.shape[0]
    q_rank, kv_rank = gq.shape[1], gkv.shape[1]
    d = wq.shape[1]
    misc_blk = z.shape[1] // d - 1
    full = lambda a: pl.BlockSpec(a.shape, lambda i: (0,) * a.ndim)
    row = lambda n: pl.BlockSpec((tm, n), lambda i: (i, 0))
    in_specs = [pl.BlockSpec((tm, d), lambda i: (i, misc_blk)), full(gq), full(wq), full(wqr),
                full(gkv), full(invf)] + [full(a) for a in extra]
    out_specs = [row(kv_rank), row(QK_ROPE)]
    out_shape = [jax.ShapeDtypeStruct((r, kv_rank), F32), jax.ShapeDtypeStruct((r, QK_ROPE), F32)]
    if sample:
        out_specs += [row(A_HEADS * kv_rank), row(d)]
        out_shape += [jax.ShapeDtypeStruct((r, A_HEADS * kv_rank), BF16), jax.ShapeDtypeStruct((r, d), BF16)]
    else:
        out_specs += [row(d), row(d), pl.BlockSpec((tm // kv_chunk, d, kv_chunk), lambda i: (i, 0, 0))]
        out_shape += [jax.ShapeDtypeStruct((r, d), BF16)] * 2 + [jax.ShapeDtypeStruct((r // kv_chunk, d, kv_chunk), BF16)]
    return pl.pallas_call(
        functools.partial(_prep_kernel, seq_len=seq_len, pos_base=pos_base, q_rank=q_rank,
                          kv_rank=kv_rank, sample=sample),
        grid=(r // tm,),
        in_specs=in_specs, out_specs=out_specs, out_shape=out_shape,
        compiler_params=_cp(("arbitrary",)),
        name="mla_prep_sample" if sample else "mla_prep_prompt",
    )(z, gq, wq, wqr, gkv, invf, *extra)


def _flash_kernel(q_ref, k_ref, vt_ref, o_ref, *, tq, kc, sub, unroll):
    qi = pl.program_id(2)
    nsub = tq // sub
    qs = [q_ref[s * sub:(s + 1) * sub, :] for s in range(nsub)]

    def group(j0, tiles, carries):
        carries = list(carries)
        chunks = sorted({c for c, _, _ in tiles})
        ks = {c: k_ref[pl.ds(pl.multiple_of((j0 + c) * kc, kc), kc), :] for c in chunks}
        vts = {c: vt_ref[j0 + c] for c in chunks}
        sts = [_dot_nt(ks[c], qs[s]) for c, s, _ in tiles]
        sts = [st if mask is None else jnp.where(mask, st, -jnp.inf) for st, (_, _, mask) in zip(sts, tiles)]
        mloc = [jnp.max(st, axis=0, keepdims=True) for st in sts]
        ps, alphas = [], []
        for (c, s, _), st, ml in zip(tiles, sts, mloc):
            m, l, acc = carries[s]
            m_new = jnp.maximum(m, ml)
            alpha = jnp.exp2(m - m_new)
            p = jnp.exp2(st - m_new)
            carries[s] = (m_new, alpha * l + jnp.sum(p, axis=0, keepdims=True), acc)
            ps.append(p.astype(BF16))
            alphas.append(alpha)
        pvs = [_dot(vts[c], p) for (c, _, _), p in zip(tiles, ps)]
        for (c, s, _), alpha, pv in zip(tiles, alphas, pvs):
            m, l, acc = carries[s]
            carries[s] = (m, l, alpha * acc + pv)
        return tuple(carries)

    def body(j, carries):
        return group(j * unroll, [(c, s, None) for c in range(unroll) for s in range(nsub)], carries)

    init = tuple((jnp.full((1, sub), -jnp.inf, F32), jnp.zeros((1, sub), F32),
                  jnp.zeros((vt_ref.shape[1], sub), F32)) for _ in range(nsub))
    per_q = tq // kc
    carries = lax.fori_loop(0, qi * (per_q // unroll), body, init)
    key = lax.broadcasted_iota(jnp.int32, (kc, sub), 0)
    qry = lax.broadcasted_iota(jnp.int32, (kc, sub), 1)
    for g0 in range(0, per_q, unroll):
        tiles = []
        for jj in range(g0, g0 + unroll):
            for s in range(nsub):
                if jj * kc > s * sub + sub - 1:
                    continue
                needs_mask = jj * kc + kc - 1 > s * sub
                tiles.append((jj, s, (key + jj * kc <= qry + s * sub) if needs_mask else None))
        carries = group(qi * per_q, tiles, carries)
    for s in range(nsub):
        _, l, acc = carries[s]
        o_ref[s * sub:(s + 1) * sub, :] = jnp.transpose(acc / l)


def _flash(q, k, vt, tq, kc, sub, unroll):
    b, t, d = q.shape
    return pl.pallas_call(
        functools.partial(_flash_kernel, tq=tq, kc=kc, sub=sub, unroll=unroll),
        grid=(b, A_HEADS, t // tq),
        in_specs=[pl.BlockSpec((None, tq, LANE), lambda b_, h, i: (b_, i, h)),
                  pl.BlockSpec((None, t, LANE), lambda b_, h, i: (b_, 0, h)),
                  pl.BlockSpec((t // kc, LANE, kc), lambda b_, h, i: (b_, h, 0))],
        out_specs=pl.BlockSpec((None, tq, LANE), lambda b_, h, i: (b_, i, h)),
        out_shape=jax.ShapeDtypeStruct((b, t, d), F32),
        compiler_params=_cp(("arbitrary", "arbitrary", "arbitrary")),
        name="flash_prompt",
    )(q, k, vt)


def _sattn_kernel(pt_ref, qlat_ref, qpe_ref, knew_ref, pnew_ref, ckv_hbm, kpe_hbm, o_ref,
                  kbuf, pbuf, sem, *, layer, n_pages, page, n_valid, ck):
    b = pl.program_id(0)
    nb = pl.num_programs(0)
    slot = b % 2

    def copies(page_of, s):
        out = []
        for p in range(n_pages):
            pg = page_of(p)
            out.append(pltpu.make_async_copy(ckv_hbm.at[layer, pg], kbuf.at[s, pl.ds(p * page, page)], sem.at[0, s]))
            out.append(pltpu.make_async_copy(kpe_hbm.at[layer, pg], pbuf.at[s, pl.ds(p * page, page)], sem.at[1, s]))
        return out

    @pl.when(b == 0)
    def _():
        for c in copies(lambda p: pt_ref[p], 0):
            c.start()

    @pl.when(b + 1 < nb)
    def _():
        for c in copies(lambda p: pt_ref[(b + 1) * n_pages + p], 1 - slot):
            c.start()

    for c in copies(lambda p: pt_ref[b * n_pages + p], slot):
        c.wait()

    qlat = qlat_ref[...]
    qpe = qpe_ref[...]
    nq = qlat.shape[0]

    def update(s, v, carry):
        m, l, acc = carry
        m_new = jnp.maximum(m, jnp.max(s, axis=-1, keepdims=True))
        alpha = jnp.exp(m - m_new)
        p = jnp.exp(s - m_new)
        l = alpha * l + jnp.sum(p, axis=-1, keepdims=True)
        acc = alpha * acc + _dot(p.astype(BF16), v)
        return m_new, l, acc

    def body(j, carry):
        off = pl.multiple_of(j * ck, ck)
        kc = kbuf[slot, pl.ds(off, ck), :].astype(BF16)
        pc = pbuf[slot, pl.ds(off, ck), :].astype(BF16)
        return update(_dot_nt(qlat, kc) + _dot_nt(qpe, pc), kc, carry)

    init = (jnp.full((nq, 1), -jnp.inf, F32), jnp.zeros((nq, 1), F32), jnp.zeros((nq, qlat.shape[1]), F32))
    carry = lax.fori_loop(0, n_pages * page // ck, body, init)
    kn = knew_ref[...].astype(BF16)
    pn = pnew_ref[...].astype(BF16)
    s = _dot_nt(qlat, kn) + _dot_nt(qpe, pn)
    tok = jnp.right_shift(lax.broadcasted_iota(jnp.int32, s.shape, 0), A_HEADS.bit_length() - 1)
    key = lax.broadcasted_iota(jnp.int32, s.shape, 1)
    s = jnp.where((key <= tok) & (key < n_valid), s, -jnp.inf)
    _, l, acc = update(s, kn, carry)
    o_ref[...] = acc / l


def _sattn(page_table, qlat, qpe, knew, pnew, cache_ckv, cache_kpe, layer, n_valid):
    bs, nq, kvr = qlat.shape
    n_pages = page_table.shape[1]
    page = cache_ckv.shape[2]
    past = n_pages * page
    ck = min(8192, past)
    blk = lambda a: pl.BlockSpec((None,) + a.shape[1:], lambda b, pt: (b,) + (0,) * (a.ndim - 1))
    return pl.pallas_call(
        functools.partial(_sattn_kernel, layer=layer, n_pages=n_pages, page=page, n_valid=n_valid, ck=ck),
        grid_spec=pltpu.PrefetchScalarGridSpec(
            num_scalar_prefetch=1, grid=(bs,),
            in_specs=[blk(qlat), blk(qpe), blk(knew), blk(pnew),
                      pl.BlockSpec(memory_space=pl.ANY), pl.BlockSpec(memory_space=pl.ANY)],
            out_specs=pl.BlockSpec((None, nq, kvr), lambda b, pt: (b, 0, 0)),
            scratch_shapes=[pltpu.VMEM((2, past, kvr), F32), pltpu.VMEM((2, past, cache_kpe.shape[3]), F32),
                            pltpu.SemaphoreType.DMA((2, 2))]),
        out_shape=jax.ShapeDtypeStruct((bs, nq, kvr), F32),
        compiler_params=_cp(("arbitrary",)),
        name="paged_attn_sample",
    )(page_table.reshape(-1), qlat, qpe, knew, pnew, cache_ckv, cache_kpe)


def _seg_sum(x, bd):
    rows, w = x.shape[0], bd.shape[0]
    n = x.shape[1] // w
    hi = x.astype(BF16).astype(F32)
    lo = x - hi
    lhs = jnp.concatenate([part[:, c * w:(c + 1) * w] for part in (hi, lo) for c in range(n)], axis=0)
    out = _dot(lhs.astype(BF16), bd)
    return jnp.concatenate([out[c * rows:(c + 1) * rows] + out[(n + c) * rows:(n + c + 1) * rows]
                            for c in range(n)], axis=1)


def _scan_kernel(r_ref, k_ref, v_ref, zz_ref, inj_ref, s0_ref,
                 mu_ref, muz_ref, w0_ref, w2_ref, a0_ref, a2_ref, kk_ref, ka_ref, rk_ref,
                 lw_ref, lb_ref, tri_ref, bd_ref,
                 ob_ref, sout_ref,
                 state, carry, *, chunks_per_seq, n_valid, use_inj):
    c = pl.program_id(1)
    L, d = r_ref.shape
    npair = d // LANE
    row = lax.broadcasted_iota(jnp.int32, (L, 1), 0)
    first = c == 0

    @pl.when(first)
    def _():
        state[...] = s0_ref[...]
        carry[...] = jnp.zeros_like(carry)

    def shift(x, idx, mu):
        if use_inj:
            prev_row = inj_ref[idx:idx + 1, 0:x.shape[1]]
        else:
            prev_row = carry[idx:idx + 1, 0:x.shape[1]]
        prev = jnp.where(row == 0, prev_row, pltpu.roll(x, 1, axis=0))
        if not use_inj:
            carry[idx:idx + 1, 0:x.shape[1]] = x[L - 1:L, :]
        return x + (prev - x) * mu

    r = shift(r_ref[...], 0, mu_ref[0:1, :])
    k = shift(k_ref[...], 1, mu_ref[1:2, :])
    v = shift(v_ref[...], 2, mu_ref[2:3, :])
    zz = shift(zz_ref[...], 3, muz_ref[...])

    wl = w0_ref[...] + _dot(jnp.tanh(zz).astype(BF16), w2_ref[...])
    sp = jnp.maximum(-wl, 0.0) + jnp.log(1.0 + jnp.exp(-jnp.abs(wl)))
    logw = -jnp.exp(-sp - 0.5)
    a_c = _sigmoid(a0_ref[...] + _dot(zz.astype(BF16), a2_ref[...]))
    kkr = k * kk_ref[...]
    bd = bd_ref[...]
    kk = kkr * lax.rsqrt(jnp.maximum(_seg_sum(kkr * kkr, bd), 1e-24))
    k = k * (1.0 + (a_c - 1.0) * ka_ref[...])
    if n_valid < L:
        valid = row < n_valid
        logw = jnp.where(valid, logw, 0.0)
        kk = jnp.where(valid, kk, 0.0)
        k = jnp.where(valid, k, 0.0)
        v = jnp.where(valid, v, 0.0)
    bonus = _seg_sum(r * k * rk_ref[...], bd) * v

    l1, l2, l3 = _split3(logw)
    tri = tri_ref[...]
    cum = _dot(tri, l1) + (_dot(tri, l2) + _dot(tri, l3))
    cum_l = cum[L - 1:L, :]
    e_fwd = jnp.exp(cum)
    e_inv = jnp.exp(-cum)
    e_end = jnp.exp(cum_l - cum)
    rt = r * e_fwd
    at = -kk * jnp.exp(cum - logw)
    b = kk * a_c
    bt, kt = b * e_inv, k * e_inv
    bh, kh = b * e_end, k * e_end
    g_l = jnp.exp(cum_l)

    lane = lax.broadcasted_iota(jnp.int32, (1, LANE), 1)
    m_a = lane < B_HEAD
    tr = lax.broadcasted_iota(jnp.int32, (2 * L, 2 * L), 0)
    tc = lax.broadcasted_iota(jnp.int32, (2 * L, 2 * L), 1)
    strict, incl, eye = tr > tc, tr >= tc, (tr == tc).astype(F32)

    def stack(x):
        return jnp.concatenate([jnp.where(m_a, x, 0.0), jnp.where(m_a, 0.0, x)], axis=0).astype(BF16)

    pairs = range(npair)
    sls = [slice(p * LANE, (p + 1) * LANE) for p in pairs]
    cat = lambda x, y_: jnp.concatenate([x, y_], axis=0)
    ar_s = [cat(stack(at[:, sl]), stack(rt[:, sl])) for sl in sls]
    bk_s = [cat(stack(bt[:, sl]), stack(kt[:, sl])) for sl in sls]
    v_s = [stack(v[:, sl]) for sl in sls]
    bkh_s = [cat(stack(bh[:, sl]), stack(kh[:, sl])) for sl in sls]
    H = 2 * L
    if H % LANE == 0:
        g = [_dot_nt(ar_s[p], bk_s[p]) for p in pairs]
        g4 = [(x[0:H, 0:H], x[0:H, H:], x[H:, 0:H], x[H:, H:]) for x in g]
    else:
        g4 = [(_dot_nt(ar_s[p][0:H], bk_s[p][0:H]), _dot_nt(ar_s[p][0:H], bk_s[p][H:]),
               _dot_nt(ar_s[p][H:], bk_s[p][0:H]), _dot_nt(ar_s[p][H:], bk_s[p][H:])) for p in pairs]
    a_ab = [jnp.where(strict, x[0], 0.0) for x in g4]
    akrk = [cat(jnp.where(strict, x[1], 0.0).astype(BF16), jnp.where(incl, x[3], 0.0).astype(BF16)) for x in g4]
    a_rb = [jnp.where(incl, x[2], 0.0).astype(BF16) for x in g4]
    t_inv = [eye + x for x in a_ab]
    pw = [x.astype(BF16) for x in a_ab]
    n = 2
    while n < L:
        pw = [_dot(x, x).astype(BF16) for x in pw]
        t_inv = [t_inv[p] + _dot(pw[p], t_inv[p].astype(BF16)) for p in pairs]
        n *= 2
    st_f = [state[p] for p in pairs]
    ars = [_dot_nt(ar_s[p], st_f[p].astype(BF16)) for p in pairs]
    av = [_dot(akrk[p], v_s[p]) for p in pairs]
    u_b = [_dot(t_inv[p].astype(BF16), (ars[p][0:H] + av[p][0:H]).astype(BF16)).astype(BF16) for p in pairs]
    y_s = [ars[p][H:] + av[p][H:] + _dot(a_rb[p], u_b[p]) for p in pairs]
    for p in pairs:
        state[p] = st_f[p] * g_l[:, sls[p]] + _dot_tn(cat(u_b[p], v_s[p]), bkh_s[p])
    y = jnp.concatenate([x[0:L] + x[L:] for x in y_s], axis=1)

    bdm = bd * (1.0 / B_HEAD)
    mean = _seg_sum(y, bdm)
    yc = y - mean
    var = _seg_sum(yc * yc, bdm)
    ob_ref[...] = yc * lax.rsqrt(var + GN_EPS) * lw_ref[...] + lb_ref[...] + bonus

    @pl.when(c == chunks_per_seq - 1)
    def _():
        sout_ref[...] = state[...]


def _scan(z, inj, s0, wts, *, n_seq, seq_len, chunk, n_valid):
    r = z.shape[0]
    d = wts[0].shape[1]
    npair = d // LANE
    cps = seq_len // chunk
    zz_blk = (5 * d + 640) // LANE
    use_inj = inj is not None
    if inj is None:
        inj = jnp.zeros((n_seq, SUBLANE, d), F32)
    tri = (jnp.arange(chunk)[:, None] >= jnp.arange(chunk)[None, :]).astype(BF16)
    bd = (jnp.arange(2 * LANE)[:, None] // B_HEAD == jnp.arange(2 * LANE)[None, :] // B_HEAD).astype(BF16)
    full = lambda a: pl.BlockSpec(a.shape, lambda b, c: (0,) * a.ndim)
    colblk = lambda j: pl.BlockSpec((chunk, d), lambda b, c: (b * cps + c, j))
    consts = list(wts) + [tri, bd]
    return pl.pallas_call(
        functools.partial(_scan_kernel, chunks_per_seq=cps, n_valid=n_valid, use_inj=use_inj),
        grid=(n_seq, cps),
        in_specs=[colblk(0), colblk(1), colblk(2),
                  pl.BlockSpec((chunk, LANE), lambda b, c: (b * cps + c, zz_blk)),
                  pl.BlockSpec((None, SUBLANE, d), lambda b, c: (b, 0, 0)),
                  pl.BlockSpec((None, npair, LANE, LANE), lambda b, c: (b, 0, 0, 0))]
                 + [full(a) for a in consts],
        out_specs=[pl.BlockSpec((chunk, d), lambda b, c: (b * cps + c, 0)),
                   pl.BlockSpec((None, npair, LANE, LANE), lambda b, c: (b, 0, 0, 0))],
        out_shape=[jax.ShapeDtypeStruct((r, d), F32),
                   jax.ShapeDtypeStruct((n_seq, npair, LANE, LANE), F32)],
        scratch_shapes=[pltpu.VMEM((npair, LANE, LANE), F32), pltpu.VMEM((SUBLANE, d), F32)],
        compiler_params=_cp(("arbitrary", "arbitrary")),
        name="rwkv_scan",
    )(z, z, z, z, inj, s0, *consts)


def _merge_kernel(oa_ref, ob_ref, ga_ref, gb_ref, x_ref, g1_ref, wo_ref, *rest, sample):
    if sample:
        wuv_ref, o_ref = rest
        kvr = wuv_ref.shape[1]
        oa = jnp.concatenate(
            [_dot(oa_ref[:, h * kvr:(h + 1) * kvr].astype(BF16), wuv_ref[h]) for h in range(A_HEADS)], axis=1)
    else:
        (o_ref,) = rest
        oa = oa_ref[...]
    merged = _sigmoid(ga_ref[...]) * oa + _sigmoid(gb_ref[...]) * ob_ref[...]
    o_ref[...] = x_ref[...] + g1_ref[...] * _dot(merged.astype(BF16), wo_ref[...])


def _merge(oa, ob, z, x, g1, wo, wuv, rows_per_batch, tm):
    r, d = x.shape
    sample = wuv is not None
    full = lambda a: pl.BlockSpec(a.shape, lambda i: (0,) * a.ndim)
    row = lambda n: pl.BlockSpec((tm, n), lambda i: (i, 0))
    in_specs = [row(oa.shape[1]), row(d), pl.BlockSpec((tm, d), lambda i: (i, 3)),
                pl.BlockSpec((tm, d), lambda i: (i, 4)), row(d), _mod_spec(g1, tm, rows_per_batch), full(wo)]
    args = [oa, ob, z, z, x, g1, wo]
    if sample:
        in_specs.append(full(wuv))
        args.append(wuv)
    return pl.pallas_call(
        functools.partial(_merge_kernel, sample=sample),
        grid=(r // tm,),
        in_specs=in_specs, out_specs=row(d),
        out_shape=jax.ShapeDtypeStruct((r, d), F32),
        compiler_params=_cp(("arbitrary",)),
        name="merge_out",
    )(*args)


def _ffn_kernel(x_ref, sc_ref, sh_ref, g2_ref, gn_ref, wu_ref, wv_ref, cw_ref, cb_ref, wd_ref, *rest,
                seq_len, sample, final):
    rest = list(rest)
    if sample:
        inj1_ref, inj2_ref = rest[:2]
        rest = rest[2:]
    if final:
        gf_ref = rest.pop(0)
    x2_ref, u_ref = rest[:2]
    rest = rest[2:]
    if final:
        y_ref = rest.pop(0)
    h_ref, acc_ref, carry_ref = rest
    i, j = pl.program_id(0), pl.program_id(1)
    tm = x_ref.shape[0]

    @pl.when(j == 0)
    def _():
        y = _rms(x_ref[...], gn_ref[...])
        h_ref[...] = (y * (1.0 + sc_ref[...]) + sh_ref[...]).astype(BF16)
        acc_ref[...] = jnp.zeros_like(acc_ref)

    h = h_ref[...]
    u = _dot(h, wu_ref[...])
    val = _dot(h, wv_ref[...])
    row = lax.broadcasted_iota(jnp.int32, (tm, 1), 0)
    r1, r2 = pltpu.roll(u, 1, axis=0), pltpu.roll(u, 2, axis=0)
    if sample:
        t = row & (seq_len - 1)
        u1 = jnp.where(t == 0, inj1_ref[...], r1)
        u2 = jnp.where(t < 2, inj2_ref[...], r2)
        u_ref[...] = u
    else:
        @pl.when(((i * tm) & (seq_len - 1)) == 0)
        def _():
            carry_ref[j] = jnp.zeros(carry_ref.shape[1:], F32)

        prev = carry_ref[j]
        u1 = jnp.where(row == 0, prev[7:8, :], r1)
        u2 = jnp.where(row == 0, prev[6:7, :], jnp.where(row == 1, prev[7:8, :], r2))
        carry_ref[j] = u[tm - SUBLANE:tm, :]
        u_ref[...] = u[tm - SUBLANE:tm, :]
    conv = cb_ref[...] + cw_ref[0:1, :] * u2 + cw_ref[1:2, :] * u1 + cw_ref[2:3, :] * u
    act = (conv * _sigmoid(conv) * val).astype(BF16)
    acc_ref[...] += _dot(act, wd_ref[...])

    @pl.when(j == pl.num_programs(1) - 1)
    def _():
        x2 = x_ref[...] + g2_ref[...] * acc_ref[...]
        x2_ref[...] = x2
        if final:
            y_ref[...] = _rms(x2, gf_ref[...])


def _ffn(x, sc, sh, g2, gn, wu, wv, cw, cb, wd, inj, gf, *, rows_per_batch, seq_len, tm, tf):
    r, d = x.shape
    f = wu.shape[1]
    sample, final = inj is not None, gf is not None
    nf = f // tf
    cst = lambda a: pl.BlockSpec(a.shape, lambda i, j: (0,) * a.ndim)
    in_specs = [pl.BlockSpec((tm, d), lambda i, j: (i, 0)),
                _mod_spec(sc, tm, rows_per_batch), _mod_spec(sh, tm, rows_per_batch),
                _mod_spec(g2, tm, rows_per_batch), cst(gn),
                pl.BlockSpec((d, tf), lambda i, j: (0, j)), pl.BlockSpec((d, tf), lambda i, j: (0, j)),
                pl.BlockSpec((3, tf), lambda i, j: (0, j)), pl.BlockSpec((1, tf), lambda i, j: (0, j)),
                pl.BlockSpec((tf, d), lambda i, j: (j, 0))]
    args = [x, sc, sh, g2, gn, wu, wv, cw, cb, wd]
    if sample:
        in_specs += [pl.BlockSpec((tm, tf), lambda i, j: (i, j))] * 2
        args += list(inj)
    if final:
        in_specs.append(cst(gf))
        args.append(gf)
    out_specs = [pl.BlockSpec((tm, d), lambda i, j: (i, 0))]
    out_shape = [jax.ShapeDtypeStruct((r, d), F32)]
    if sample:
        out_specs.append(pl.BlockSpec((tm, tf), lambda i, j: (i, j)))
        out_shape.append(jax.ShapeDtypeStruct((r, f), F32))
    else:
        out_specs.append(pl.BlockSpec((None, SUBLANE, tf), lambda i, j: (i, 0, j)))
        out_shape.append(jax.ShapeDtypeStruct((r // tm, SUBLANE, f), F32))
    if final:
        out_specs.append(pl.BlockSpec((tm, d), lambda i, j: (i, 0)))
        out_shape.append(jax.ShapeDtypeStruct((r, d), F32))
    return pl.pallas_call(
        functools.partial(_ffn_kernel, seq_len=seq_len, sample=sample, final=final),
        grid=(r // tm, nf),
        in_specs=in_specs, out_specs=out_specs, out_shape=out_shape,
        scratch_shapes=[pltpu.VMEM((tm, d), BF16), pltpu.VMEM((tm, d), F32),
                        pltpu.VMEM((nf, SUBLANE, tf), F32)],
        compiler_params=_cp(("arbitrary", "arbitrary")),
        name="conv_ffn",
    )(*args)


def _pad_cols(a, n):
    return jnp.pad(a, ((0, 0), (0, n - a.shape[1])))


def _rot(a):
    half = a.shape[-1] // 2
    return jnp.concatenate([-a[..., half:], a[..., :half]], axis=-1)


def _layer_weights(l, w_in, mu_shift, w_uq, w_uk, w_uv, w2, a2, d, q_rank, kv_rank):
    dd = B_HEAD
    o_zb = q_rank + kv_rank + QK_ROPE
    o_r, o_zw, o_k, o_v, o_za = 0, d, d + dd, 2 * d + dd, 3 * d + dd
    n_zb = 3 * d + 2 * dd
    wi = w_in[l]
    zb = wi[:, o_zb:o_zb + n_zb]
    kpe_w = wi[:, q_rank + kv_rank:o_zb]
    misc = jnp.concatenate([wi[:, :q_rank + kv_rank], zb[:, o_zw:o_zw + dd], zb[:, o_za:o_za + dd],
                            _pad_cols(kpe_w, LANE), _pad_cols(_rot(kpe_w), LANE)], axis=1)
    w_z = jnp.concatenate([zb[:, o_r:o_r + d], zb[:, o_k:o_k + d], zb[:, o_v:o_v + d],
                           wi[:, o_zb + n_zb:o_zb + n_zb + 2 * d], misc], axis=1).astype(BF16)
    mu = mu_shift[l]
    mu_rkv = jnp.stack([mu[o_r:o_r + d], mu[o_k:o_k + d], mu[o_v:o_v + d]])
    mu_zz = jnp.concatenate([mu[o_zw:o_zw + dd], mu[o_za:o_za + dd]])[None]
    wq3 = w_uq[l].reshape(q_rank, A_HEADS, QK_NOPE + QK_ROPE)
    nope, rope = wq3[..., :QK_NOPE], wq3[..., QK_NOPE:]
    zpad = jnp.zeros((q_rank, A_HEADS, LANE - QK_NOPE - QK_ROPE), F32)
    wq = jnp.concatenate([rope, nope, zpad], axis=-1).reshape(q_rank, A_HEADS * LANE).astype(BF16)
    wqr = jnp.concatenate([_rot(rope), jnp.zeros_like(nope), zpad], axis=-1).reshape(q_rank, A_HEADS * LANE).astype(BF16)
    zk = jnp.zeros((kv_rank, A_HEADS, QK_ROPE), F32)
    wk = jnp.concatenate([zk, w_uk[l], zk], axis=-1).reshape(kv_rank, A_HEADS * LANE).astype(BF16)
    wv = w_uv[l].reshape(kv_rank, -1).T.astype(BF16)
    ukt = jnp.transpose(w_uk[l], (1, 2, 0))
    zr = jnp.zeros((A_HEADS, QK_ROPE, kv_rank), F32)
    wabs = jnp.concatenate([zr, ukt, zr], axis=1).astype(BF16)
    wuv_h = jnp.transpose(w_uv[l], (1, 0, 2)).astype(BF16)
    zrow = jnp.zeros((dd, d), F32)
    w2p = jnp.concatenate([w2[l], zrow], axis=0).astype(BF16)
    a2p = jnp.concatenate([zrow, a2[l]], axis=0).astype(BF16)
    return dict(w_z=w_z, mu_rkv=mu_rkv, mu_zz=mu_zz, wq=wq, wqr=wqr, wk=wk, wv=wv, wabs=wabs,
                wuv_h=wuv_h, w2p=w2p, a2p=a2p)


def _unperm_shift(zrow, d):
    dd = B_HEAD
    m = 5 * d
    zw, za = zrow[:, m + 640:m + 640 + dd], zrow[:, m + 640 + dd:m + 640 + 2 * dd]
    return jnp.concatenate([zrow[:, 0:d], zw, zrow[:, d:2 * d], zrow[:, 2 * d:3 * d], za], axis=1)


def _shift_inj(state_shift_l, d):
    dd = B_HEAD
    s = state_shift_l
    r, zw, k, v, za = s[:, 0:d], s[:, d:d + dd], s[:, d + dd:2 * d + dd], s[:, 2 * d + dd:3 * d + dd], s[:, 3 * d + dd:]
    zz = jnp.concatenate([zw, za, jnp.zeros((s.shape[0], d - 2 * dd), F32)], axis=1)
    rows = jnp.stack([r, k, v, zz], axis=1)
    return jnp.concatenate([rows, jnp.zeros((s.shape[0], SUBLANE - 4, d), F32)], axis=1)


def _pair_state(s):
    n, h = s.shape[:2]
    s = s.reshape(n, h // 2, 2, B_HEAD, B_HEAD)
    z = jnp.zeros_like(s[:, :, 0])
    top = jnp.concatenate([s[:, :, 0], z], axis=-1)
    bot = jnp.concatenate([z, s[:, :, 1]], axis=-1)
    return jnp.concatenate([top, bot], axis=-2)


def _unpair_state(sp):
    n, hp = sp.shape[:2]
    a = sp[:, :, :B_HEAD, :B_HEAD]
    b = sp[:, :, B_HEAD:, B_HEAD:]
    return jnp.stack([a, b], axis=2).reshape(n, 2 * hp, B_HEAD, B_HEAD)


def kernel(x_prompt, x_sample, c_prompt, c_sample, cache_ckv, cache_kpe, state_wkv, state_shift, state_conv, page_table, w_ada, b_ada, g_mix, g_ffn, w_in, g_q, w_uq, g_kv, w_uk, w_uv, mu_shift, w0, w2, a0, a2, k_k, k_a, r_k, lnx_w, lnx_b, w_out, w_up, conv_w, conv_b, w_down, g_final):
    bp, tp, d = x_prompt.shape
    bs, ts, _ = x_sample.shape
    depth = w_in.shape[0]
    q_rank, kv_rank = g_q.shape[1], g_kv.shape[1]
    d_ff = w_down.shape[1]
    n_pages, page = page_table.shape[1], cache_ckv.shape[2]
    past_len = n_pages * page
    heads_b = d // B_HEAD
    rp, rs = bp * tp, bs * S_PAD

    tm_p = min(512, tp)
    tm_s = min(256, rs)
    tq = min(1024, tp)
    kc, fl_sub, fl_unroll = min(128, tp), min(256, tq), 8
    chunk_p = min(64, tp)
    tf = d_ff // 2

    mod = _ada(jnp.concatenate([c_prompt, c_sample], axis=0), w_ada, b_ada)
    invf = ROPE_THETA ** (-jnp.arange(0, QK_ROPE, 2, dtype=F32) / QK_ROPE)
    invf = jnp.tile(invf, LANE // invf.shape[0])[None]

    xp = x_prompt.reshape(rp, d)
    xs = jnp.pad(x_sample, ((0, 0), (0, S_PAD - ts), (0, 0))).reshape(rs, d)
    new_p = [[] for _ in range(5)]
    new_s = [[] for _ in range(5)]
    yp = ys = None
    for l in range(depth):
        w = _layer_weights(l, w_in, mu_shift, w_uq, w_uk, w_uv, w2, a2, d, q_rank, kv_rank)
        mods_p = [m[:, None, :] for m in jnp.split(mod[l, :bp], 6, axis=-1)]
        mods_s = [jnp.repeat(m, S_PAD, axis=0) for m in jnp.split(mod[l, bp:], 6, axis=-1)]
        row1 = lambda a: a[l][None]
        scan_w = [w['mu_rkv'], w['mu_zz'], row1(w0), w['w2p'], row1(a0), w['a2p'], row1(k_k), row1(k_a),
                  r_k[l].reshape(1, d), row1(lnx_w), row1(lnx_b)]
        wu, wv_up = w_up[l][:, :d_ff].astype(BF16), w_up[l][:, d_ff:].astype(BF16)
        wd, wo = w_down[l].astype(BF16), w_out[l].astype(BF16)
        gf = g_final[None] if l == depth - 1 else None

        sh1, sc1, g1, sh2, sc2, g2 = mods_p
        z = _normproj(xp, sc1, sh1, row1(g_mix), w['w_z'], tp, tm_p, 1536)
        ckv, kpe, q, k, vt = _prep(z, row1(g_q), w['wq'], w['wqr'], row1(g_kv), invf, [w['wk'], w['wv']],
                                   seq_len=tp, pos_base=0, sample=False, tm=tm_p, kv_chunk=kc)
        oa = _flash(q.reshape(bp, tp, d), k.reshape(bp, tp, d), vt, tq, kc, fl_sub, fl_unroll).reshape(rp, d)
        ob, sp = _scan(z, None, jnp.zeros((bp, heads_b // 2, LANE, LANE), F32), scan_w,
                       n_seq=bp, seq_len=tp, chunk=chunk_p, n_valid=chunk_p)
        x1 = _merge(oa, ob, z, xp, g1, wo, None, tp, tm_p)
        outs = _ffn(x1, sc2, sh2, g2, row1(g_ffn), wu, wv_up, conv_w[l], row1(conv_b), wd, None, gf,
                    rows_per_batch=tp, seq_len=tp, tm=tm_p, tf=tf)
        xp, utail = outs[0], outs[1]
        if gf is not None:
            yp = outs[2]
        zlast = z.reshape(bp, tp, -1)[:, tp - 1]
        new_p[0].append(ckv.reshape(bp, tp, kv_rank))
        new_p[1].append(kpe.reshape(bp, tp, QK_ROPE))
        new_p[2].append(_unpair_state(sp))
        new_p[3].append(_unperm_shift(zlast, d))
        new_p[4].append(utail.reshape(bp, tp // tm_p, SUBLANE, d_ff)[:, -1, SUBLANE - 2:])

        sh1, sc1, g1, sh2, sc2, g2 = mods_s
        z = _normproj(xs, sc1, sh1, row1(g_mix), w['w_z'], rs, tm_s, 1536)
        ckv, kpe, qlat, qpe = _prep(z, row1(g_q), w['wq'], w['wqr'], row1(g_kv), invf, [w['wabs']],
                                    seq_len=S_PAD, pos_base=past_len, sample=True, tm=tm_s)
        qlat = qlat.reshape(bs, S_PAD, A_HEADS, kv_rank)[:, :ts].reshape(bs, ts * A_HEADS, kv_rank)
        qpe = qpe.reshape(bs, S_PAD, A_HEADS, LANE)[:, :ts, :, :QK_ROPE].reshape(bs, ts * A_HEADS, QK_ROPE)
        ckv3, kpe3 = ckv.reshape(bs, S_PAD, kv_rank), kpe.reshape(bs, S_PAD, QK_ROPE)
        olat = _sattn(page_table, qlat, qpe, ckv3, kpe3, cache_ckv, cache_kpe, l, ts)
        olat = jnp.pad(olat.reshape(bs, ts, A_HEADS * kv_rank), ((0, 0), (0, S_PAD - ts), (0, 0)))
        ob, sp = _scan(z, _shift_inj(state_shift[l], d), _pair_state(state_wkv[l]), scan_w,
                       n_seq=bs, seq_len=S_PAD, chunk=S_PAD, n_valid=ts)
        x1 = _merge(olat.reshape(rs, -1), ob, z, xs, g1, wo, w['wuv_h'], rs, tm_s)
        cp = state_conv[l]
        zc = jnp.zeros((bs, S_PAD, d_ff), F32)
        inj1 = zc.at[:, 0].set(cp[:, 1]).reshape(rs, d_ff)
        inj2 = zc.at[:, 0].set(cp[:, 0]).at[:, 1].set(cp[:, 1]).reshape(rs, d_ff)
        outs = _ffn(x1, sc2, sh2, g2, row1(g_ffn), wu, wv_up, conv_w[l], row1(conv_b), wd, (inj1, inj2), gf,
                    rows_per_batch=rs, seq_len=S_PAD, tm=tm_s, tf=tf)
        xs, u_full = outs[0], outs[1]
        if gf is not None:
            ys = outs[2]
        new_s[0].append(ckv3[:, :ts])
        new_s[1].append(kpe3[:, :ts])
        new_s[2].append(_unpair_state(sp))
        new_s[3].append(_unperm_shift(z.reshape(bs, S_PAD, -1)[:, ts - 1], d))
        new_s[4].append(u_full.reshape(bs, S_PAD, d_ff)[:, ts - 2:ts])

    stack = lambda u: jnp.stack(u, axis=0)
    return (yp.reshape(bp, tp, d), ys.reshape(bs, S_PAD, d)[:, :ts],
            stack(new_p[0]), stack(new_p[1]), stack(new_p[2]), stack(new_p[3]), stack(new_p[4]),
            stack(new_s[0]), stack(new_s[1]), stack(new_s[2]), stack(new_s[3]), stack(new_s[4]))
```

```python
import functools

import jax
import jax.numpy as jnp
from jax import lax
from jax.experimental import pallas as pl
from jax.experimental.pallas import tpu as pltpu

F32, BF16 = jnp.float32, jnp.bfloat16

EPS = 1e-6
GN_EPS = 64e-5
ROPE_THETA = 10000.0
A_HEADS = 8
QK_NOPE = 64
QK_ROPE = 32
B_HEAD = 64
ATTN_SCALE = (QK_NOPE + QK_ROPE) ** -0.5
LOG2E = 1.4426950408889634

LANE = 128
SUBLANE = 8
V7X_VMEM_LIMIT = 56 * 1024 * 1024
V7X_MXU_DEPTH = 256
S_PAD = 8


def _cp(sem):
    return pltpu.CompilerParams(dimension_semantics=sem, vmem_limit_bytes=V7X_VMEM_LIMIT)


def _dot(a, b):
    return jnp.dot(a, b, preferred_element_type=F32)


def _dot_nt(a, b):
    return lax.dot_general(a, b, (((1,), (1,)), ((), ())), preferred_element_type=F32)


def _dot_tn(a, b):
    return lax.dot_general(a, b, (((0,), (0,)), ((), ())), preferred_element_type=F32)


def _split2(x):
    hi = x.astype(BF16)
    return hi, (x - hi.astype(F32)).astype(BF16)


def _split3(x):
    hi = x.astype(BF16)
    r = x - hi.astype(F32)
    mid = r.astype(BF16)
    return hi, mid, (r - mid.astype(F32)).astype(BF16)


def _dot3(a, b):
    ah, al = _split2(a)
    bh, bl = _split2(b)
    return _dot(ah, bh) + (_dot(ah, bl) + _dot(al, bh))


def _sigmoid(x):
    return 1.0 / (1.0 + jnp.exp(-x))


def _rms(x, g):
    return x * lax.rsqrt(jnp.mean(x * x, axis=-1, keepdims=True) + EPS) * g


def _mod_spec(arr, tm, rows_per_batch):
    d = arr.shape[-1]
    if arr.ndim == 3:
        bpb = rows_per_batch // tm
        return pl.BlockSpec((None, 1, d), lambda i, *_: (i // bpb, 0, 0))
    return pl.BlockSpec((tm, d), lambda i, *_: (i, 0))


def _ada_kernel(c_ref, w_ref, b_ref, o_ref):
    c = c_ref[...]
    s = (c * _sigmoid(c)).astype(BF16)
    o_ref[...] = _dot(s, w_ref[...].astype(BF16)) + b_ref[...]


def _ada(c_all, w_ada, b_ada, tn=1536):
    depth, d, n = w_ada.shape
    bc = c_all.shape[0]
    return pl.pallas_call(
        _ada_kernel,
        grid=(depth, n // tn),
        in_specs=[pl.BlockSpec((bc, d), lambda l, j: (0, 0)),
                  pl.BlockSpec((None, d, tn), lambda l, j: (l, 0, j)),
                  pl.BlockSpec((None, 1, tn), lambda l, j: (l, 0, j))],
        out_specs=pl.BlockSpec((None, bc, tn), lambda l, j: (l, 0, j)),
        out_shape=jax.ShapeDtypeStruct((depth, bc, n), F32),
        compiler_params=_cp(("arbitrary", "arbitrary")),
        name="ada",
    )(c_all, w_ada, b_ada.reshape(depth, 1, n))


def _normproj_kernel(x_ref, sc_ref, sh_ref, g_ref, w_ref, o_ref, h_ref):
    @pl.when(pl.program_id(1) == 0)
    def _():
        y = _rms(x_ref[...], g_ref[...])
        h_ref[...] = (y * (1.0 + sc_ref[...]) + sh_ref[...]).astype(BF16)

    o_ref[...] = _dot(h_ref[...], w_ref[...])


def _normproj(x, sc, sh, g, w, rows_per_batch, tm, tn):
    r, d = x.shape
    n = w.shape[1]
    return pl.pallas_call(
        _normproj_kernel,
        grid=(r // tm, n // tn),
        in_specs=[pl.BlockSpec((tm, d), lambda i, j: (i, 0)),
                  _mod_spec(sc, tm, rows_per_batch), _mod_spec(sh, tm, rows_per_batch),
                  pl.BlockSpec((1, d), lambda i, j: (0, 0)),
                  pl.BlockSpec((d, tn), lambda i, j: (0, j))],
        out_specs=pl.BlockSpec((tm, tn), lambda i, j: (i, j)),
        out_shape=jax.ShapeDtypeStruct((r, n), F32),
        scratch_shapes=[pltpu.VMEM((tm, d), BF16)],
        compiler_params=_cp(("arbitrary", "arbitrary")),
        name="normproj",
    )(x, sc, sh, g, w)


def _prep_kernel(misc_ref, gq_ref, wq_ref, wqr_ref, gkv_ref, invf_ref, *rest,
                 seq_len, pos_base, q_rank, kv_rank, sample):
    tm = misc_ref.shape[0]
    i = pl.program_id(0)
    row = lax.broadcasted_iota(jnp.int32, (tm, LANE), 0) + i * tm
    pos = (pos_base + (row & (seq_len - 1))).astype(F32)
    ang = pos * invf_ref[...]
    cosv, sinv = jnp.cos(ang), jnp.sin(ang)
    lane = lax.broadcasted_iota(jnp.int32, (tm, LANE), 1)
    is_pe = lane < QK_ROPE
    ck = jnp.where(is_pe, cosv, 0.0)
    sk = jnp.where(is_pe, sinv, 0.0)
    qscale = ATTN_SCALE if sample else ATTN_SCALE * LOG2E
    cq = jnp.where(is_pe, cosv, jnp.where(lane < QK_ROPE + QK_NOPE, 1.0, 0.0)) * qscale
    sq = sk * qscale

    qn = _rms(misc_ref[:, 0:q_rank], gq_ref[...]).astype(BF16)
    q = _dot(qn, wq_ref[...])
    qr = _dot(qn, wqr_ref[...])
    ckv = _rms(misc_ref[:, q_rank:q_rank + kv_rank], gkv_ref[...])
    ckv_b = ckv.astype(BF16)
    o_kpe = q_rank + kv_rank + LANE
    kpe = misc_ref[:, o_kpe:o_kpe + LANE] * ck + misc_ref[:, o_kpe + LANE:o_kpe + 2 * LANE] * sk

    if sample:
        wabs_ref, ckv_ref, kpe_ref, qlat_ref, qpe_ref = rest
        for h in range(A_HEADS):
            sl = slice(h * LANE, (h + 1) * LANE)
            qh = (q[:, sl] * cq + qr[:, sl] * sq).astype(BF16)
            qpe_ref[:, sl] = qh
            qlat_ref[:, h * kv_rank:(h + 1) * kv_rank] = _dot(qh, wabs_ref[h]).astype(BF16)
    else:
        wk_ref, wvt_ref, ckv_ref, kpe_ref, q_ref, k_ref, vt_ref = rest
        kn = _dot(ckv_b, wk_ref[...])
        for h in range(A_HEADS):
            sl = slice(h * LANE, (h + 1) * LANE)
            q_ref[:, sl] = (q[:, sl] * cq + qr[:, sl] * sq).astype(BF16)
            k_ref[:, sl] = (kn[:, sl] + kpe).astype(BF16)
        vt = _dot_nt(wvt_ref[...], ckv_b)
        kc = vt_ref.shape[2]
        for c in range(tm // kc):
            vt_ref[c] = vt[:, c * kc:(c + 1) * kc].astype(BF16)
    ckv_ref[...] = ckv
    kpe_ref[...] = kpe[:, 0:QK_ROPE]


def _prep(z, gq, wq, wqr, gkv, invf, extra, *, seq_len, pos_base, sample, tm, kv_chunk=None):
    r = z.shape[0]
    q_rank, kv_rank = gq.shape[1], gkv.shape[1]
    d = wq.shape[1]
    misc_blk = z.shape[1] // d - 1
    full = lambda a: pl.BlockSpec(a.shape, lambda i: (0,) * a.ndim)
    row = lambda n: pl.BlockSpec((tm, n), lambda i: (i, 0))
    in_specs = [pl.BlockSpec((tm, d), lambda i: (i, misc_blk)), full(gq), full(wq), full(wqr),
                full(gkv), full(invf)] + [full(a) for a in extra]
    out_specs = [row(kv_rank), row(QK_ROPE)]
    out_shape = [jax.ShapeDtypeStruct((r, kv_rank), F32), jax.ShapeDtypeStruct((r, QK_ROPE), F32)]
    if sample:
        out_specs += [row(A_HEADS * kv_rank), row(d)]
        out_shape += [jax.ShapeDtypeStruct((r, A_HEADS * kv_rank), BF16), jax.ShapeDtypeStruct((r, d), BF16)]
    else:
        out_specs += [row(d), row(d), pl.BlockSpec((tm // kv_chunk, d, kv_chunk), lambda i: (i, 0, 0))]
        out_shape += [jax.ShapeDtypeStruct((r, d), BF16)] * 2 + [jax.ShapeDtypeStruct((r // kv_chunk, d, kv_chunk), BF16)]
    return pl.pallas_call(
        functools.partial(_prep_kernel, seq_len=seq_len, pos_base=pos_base, q_rank=q_rank,
                          kv_rank=kv_rank, sample=sample),
        grid=(r // tm,),
        in_specs=in_specs, out_specs=out_specs, out_shape=out_shape,
        compiler_params=_cp(("arbitrary",)),
        name="mla_prep_sample" if sample else "mla_prep_prompt",
    )(z, gq, wq, wqr, gkv, invf, *extra)


def _flash_kernel(q_ref, k_ref, vt_ref, o_ref, *, tq, kc, sub, unroll, pv_depth):
    qi = pl.program_id(2)
    nsub = tq // sub
    qs = [q_ref[s * sub:(s + 1) * sub, :] for s in range(nsub)]

    def group(j0, tiles, carries):
        carries = list(carries)
        chunks = sorted({c for c, _, _ in tiles})
        ks = {c: k_ref[pl.ds(pl.multiple_of((j0 + c) * kc, kc), kc), :] for c in chunks}
        vts = {c: vt_ref[j0 + c] for c in chunks}
        sts = [_dot_nt(ks[c], qs[s]) for c, s, _ in tiles]
        sts = [st if mask is None else jnp.where(mask, st, -jnp.inf) for st, (_, _, mask) in zip(sts, tiles)]
        mloc = [jnp.max(st, axis=0, keepdims=True) for st in sts]
        units = []
        for s in range(nsub):
            idx = [i for i, (_, s_, _) in enumerate(tiles) if s_ == s]
            units += [(s, idx[i:i + pv_depth]) for i in range(0, len(idx), pv_depth)]
        ps, alphas = [], []
        for s, idx in units:
            m, l, acc = carries[s]
            m_new = m
            for i in idx:
                m_new = jnp.maximum(m_new, mloc[i])
            alpha = jnp.exp2(m - m_new)
            p = [jnp.exp2(sts[i] - m_new) for i in idx]
            l = alpha * l
            for x in p:
                l = l + jnp.sum(x, axis=0, keepdims=True)
            carries[s] = (m_new, l, acc)
            ps.append(jnp.concatenate([x.astype(BF16) for x in p], axis=0))
            alphas.append(alpha)
        pvs = [_dot(jnp.concatenate([vts[tiles[i][0]] for i in idx], axis=1), p)
               for (_, idx), p in zip(units, ps)]
        for (s, _), alpha, pv in zip(units, alphas, pvs):
            m, l, acc = carries[s]
            carries[s] = (m, l, alpha * acc + pv)
        return tuple(carries)

    def body(j, carries):
        return group(j * unroll, [(c, s, None) for c in range(unroll) for s in range(nsub)], carries)

    init = tuple((jnp.full((1, sub), -jnp.inf, F32), jnp.zeros((1, sub), F32),
                  jnp.zeros((vt_ref.shape[1], sub), F32)) for _ in range(nsub))
    per_q = tq // kc
    carries = lax.fori_loop(0, qi * (per_q // unroll), body, init)
    key = lax.broadcasted_iota(jnp.int32, (kc, sub), 0)
    qry = lax.broadcasted_iota(jnp.int32, (kc, sub), 1)
    for g0 in range(0, per_q, unroll):
        tiles = []
        for jj in range(g0, g0 + unroll):
            for s in range(nsub):
                if jj * kc > s * sub + sub - 1:
                    continue
                needs_mask = jj * kc + kc - 1 > s * sub
                tiles.append((jj, s, (key + jj * kc <= qry + s * sub) if needs_mask else None))
        carries = group(qi * per_q, tiles, carries)
    for s in range(nsub):
        _, l, acc = carries[s]
        o_ref[s * sub:(s + 1) * sub, :] = jnp.transpose(acc / l)


def _flash(q, k, vt, tq, kc, sub, unroll):
    b, t, d = q.shape
    return pl.pallas_call(
        functools.partial(_flash_kernel, tq=tq, kc=kc, sub=sub, unroll=unroll,
                          pv_depth=max(1, V7X_MXU_DEPTH // kc)),
        grid=(b, A_HEADS, t // tq),
        in_specs=[pl.BlockSpec((None, tq, LANE), lambda b_, h, i: (b_, i, h)),
                  pl.BlockSpec((None, t, LANE), lambda b_, h, i: (b_, 0, h)),
                  pl.BlockSpec((t // kc, LANE, kc), lambda b_, h, i: (b_, h, 0))],
        out_specs=pl.BlockSpec((None, tq, LANE), lambda b_, h, i: (b_, i, h)),
        out_shape=jax.ShapeDtypeStruct((b, t, d), F32),
        compiler_params=_cp(("arbitrary", "arbitrary", "arbitrary")),
        name="flash_prompt",
    )(q, k, vt)


def _sattn_kernel(pt_ref, qlat_ref, qpe_ref, knew_ref, pnew_ref, ckv_hbm, kpe_hbm, o_ref,
                  kbuf, pbuf, sem, *, layer, n_pages, page, n_valid, ck):
    b = pl.program_id(0)
    nb = pl.num_programs(0)
    slot = b % 2

    def copies(page_of, s):
        out = []
        for p in range(n_pages):
            pg = page_of(p)
            out.append(pltpu.make_async_copy(ckv_hbm.at[layer, pg], kbuf.at[s, pl.ds(p * page, page)], sem.at[0, s]))
            out.append(pltpu.make_async_copy(kpe_hbm.at[layer, pg], pbuf.at[s, p], sem.at[1, s]))
        return out

    @pl.when(b == 0)
    def _():
        for c in copies(lambda p: pt_ref[p], 0):
            c.start()

    @pl.when(b + 1 < nb)
    def _():
        for c in copies(lambda p: pt_ref[(b + 1) * n_pages + p], 1 - slot):
            c.start()

    for c in copies(lambda p: pt_ref[b * n_pages + p], slot):
        c.wait()

    qlat = qlat_ref[...]
    qpe = qpe_ref[...]
    ppt = ck // page
    tiles = range(n_pages // ppt)
    kcs = [kbuf[slot, t * ck:(t + 1) * ck, :].astype(BF16) for t in tiles]
    s_pe = [jnp.concatenate([_dot(qpe, pbuf[slot, p].astype(BF16)) for p in range(t * ppt, (t + 1) * ppt)], axis=1)
            for t in tiles]
    ss = [_dot_nt(qlat, kcs[t]) + s_pe[t] for t in tiles]
    kn = knew_ref[...].astype(BF16)
    pn = pnew_ref[...].astype(BF16)
    s_new = _dot_nt(qlat, kn) + _dot_nt(qpe, pn)
    tok = jnp.right_shift(lax.broadcasted_iota(jnp.int32, s_new.shape, 0), A_HEADS.bit_length() - 1)
    key = lax.broadcasted_iota(jnp.int32, s_new.shape, 1)
    s_new = jnp.where((key <= tok) & (key < n_valid), s_new, -jnp.inf)
    m = jnp.max(s_new, axis=-1, keepdims=True)
    for s in ss:
        m = jnp.maximum(m, jnp.max(s, axis=-1, keepdims=True))
    p_new = jnp.exp(s_new - m)
    ps = [jnp.exp(s - m) for s in ss]
    l = jnp.sum(p_new, axis=-1, keepdims=True)
    for p in ps:
        l = l + jnp.sum(p, axis=-1, keepdims=True)
    acc = _dot(p_new.astype(BF16), kn)
    for t in tiles:
        acc = acc + _dot(ps[t].astype(BF16), kcs[t])
    o_ref[...] = acc / l


def _sattn(page_table, qlat, qpe, knew, pnew, cache_ckv, cache_kpe_t, layer, n_valid):
    bs, nq, kvr = qlat.shape
    n_pages = page_table.shape[1]
    page = cache_ckv.shape[2]
    past = n_pages * page
    ck = min(512, past)
    blk = lambda a: pl.BlockSpec((None,) + a.shape[1:], lambda b, pt: (b,) + (0,) * (a.ndim - 1))
    return pl.pallas_call(
        functools.partial(_sattn_kernel, layer=layer, n_pages=n_pages, page=page, n_valid=n_valid, ck=ck),
        grid_spec=pltpu.PrefetchScalarGridSpec(
            num_scalar_prefetch=1, grid=(bs,),
            in_specs=[blk(qlat), blk(qpe), blk(knew), blk(pnew),
                      pl.BlockSpec(memory_space=pl.ANY), pl.BlockSpec(memory_space=pl.ANY)],
            out_specs=pl.BlockSpec((None, nq, kvr), lambda b, pt: (b, 0, 0)),
            scratch_shapes=[pltpu.VMEM((2, past, kvr), F32),
                            pltpu.VMEM((2, n_pages) + cache_kpe_t.shape[2:], F32),
                            pltpu.SemaphoreType.DMA((2, 2))]),
        out_shape=jax.ShapeDtypeStruct((bs, nq, kvr), F32),
        compiler_params=_cp(("arbitrary",)),
        name="paged_attn_sample",
    )(page_table.reshape(-1), qlat, qpe, knew, pnew, cache_ckv, cache_kpe_t)


def _seg_sum(x, bd):
    rows, w = x.shape[0], bd.shape[0]
    n = x.shape[1] // w
    hi = x.astype(BF16).astype(F32)
    lo = x - hi
    lhs = jnp.concatenate([part[:, c * w:(c + 1) * w] for part in (hi, lo) for c in range(n)], axis=0)
    out = _dot(lhs.astype(BF16), bd)
    return jnp.concatenate([out[c * rows:(c + 1) * rows] + out[(n + c) * rows:(n + c + 1) * rows]
                            for c in range(n)], axis=1)


def _scan_kernel(r_ref, k_ref, v_ref, zz_ref, inj_ref, s0_ref,
                 mu_ref, muz_ref, w0_ref, w2_ref, a0_ref, a2_ref, kk_ref, ka_ref, rk_ref,
                 lw_ref, lb_ref, tri_ref, bd_ref,
                 ob_ref, sout_ref,
                 state, carry, *, steps_per_seq, cpb, n_valid, use_inj):
    c = pl.program_id(1)
    rows, d = r_ref.shape
    L = rows // cpb
    npair = d // LANE
    row = lax.broadcasted_iota(jnp.int32, (rows, 1), 0)
    first = c == 0

    @pl.when(first)
    def _():
        zero = jnp.zeros((B_HEAD, B_HEAD), F32)
        for p in range(npair):
            state[p] = jnp.concatenate([jnp.concatenate([s0_ref[2 * p], zero], axis=1),
                                        jnp.concatenate([zero, s0_ref[2 * p + 1]], axis=1)], axis=0)
        carry[...] = jnp.zeros_like(carry)

    def shift(x, idx, mu):
        if use_inj:
            prev_row = inj_ref[idx:idx + 1, 0:x.shape[1]]
        else:
            prev_row = carry[idx:idx + 1, 0:x.shape[1]]
        prev = jnp.where(row == 0, prev_row, pltpu.roll(x, 1, axis=0))
        if not use_inj:
            carry[idx:idx + 1, 0:x.shape[1]] = x[rows - 1:rows, :]
        return x + (prev - x) * mu

    r = shift(r_ref[...], 0, mu_ref[0:1, :])
    k = shift(k_ref[...], 1, mu_ref[1:2, :])
    v = shift(v_ref[...], 2, mu_ref[2:3, :])
    zz = shift(zz_ref[...], 3, muz_ref[...])

    wl = w0_ref[...] + _dot(jnp.tanh(zz).astype(BF16), w2_ref[...])
    sp = jnp.maximum(-wl, 0.0) + jnp.log(1.0 + jnp.exp(-jnp.abs(wl)))
    logw = -jnp.exp(-sp - 0.5)
    a_c = _sigmoid(a0_ref[...] + _dot(zz.astype(BF16), a2_ref[...]))
    kkr = k * kk_ref[...]
    bd = bd_ref[...]
    kk = kkr * lax.rsqrt(jnp.maximum(_seg_sum(kkr * kkr, bd), 1e-24))
    k = k * (1.0 + (a_c - 1.0) * ka_ref[...])
    if n_valid < L:
        assert cpb == 1
        valid = row < n_valid
        logw = jnp.where(valid, logw, 0.0)
        kk = jnp.where(valid, kk, 0.0)
        k = jnp.where(valid, k, 0.0)
        v = jnp.where(valid, v, 0.0)
    bonus = _seg_sum(r * k * rk_ref[...], bd) * v

    l1, l2, l3 = _split3(logw)
    tri = tri_ref[...]
    cum = _dot(tri, l1) + (_dot(tri, l2) + _dot(tri, l3))
    cum_ends = [cum[(q + 1) * L - 1:(q + 1) * L, :] for q in range(cpb)]
    cum_end = cum_ends[0] if cpb == 1 else jnp.concatenate(
        [jnp.broadcast_to(x, (L, d)) for x in cum_ends], axis=0)
    e_fwd = jnp.exp(cum)
    e_inv = jnp.exp(-cum)
    e_end = jnp.exp(cum_end - cum)
    rt = r * e_fwd
    at = -kk * jnp.exp(cum - logw)
    b = kk * a_c
    bt, kt = b * e_inv, k * e_inv
    bh, kh = b * e_end, k * e_end
    g_ls = [jnp.exp(x) for x in cum_ends]

    lane = lax.broadcasted_iota(jnp.int32, (1, LANE), 1)
    m_a = lane < B_HEAD
    tr = lax.broadcasted_iota(jnp.int32, (2 * L, 2 * L), 0)
    tc = lax.broadcasted_iota(jnp.int32, (2 * L, 2 * L), 1)
    strict, incl, eye = tr > tc, tr >= tc, (tr == tc).astype(F32)

    def stack(x):
        return jnp.concatenate([jnp.where(m_a, x, 0.0), jnp.where(m_a, 0.0, x)], axis=0).astype(BF16)

    pairs = range(npair)
    sls = [slice(p * LANE, (p + 1) * LANE) for p in pairs]
    cat = lambda x, y_: jnp.concatenate([x, y_], axis=0)
    H = 2 * L

    def prepare(q):
        rs = slice(q * L, (q + 1) * L)
        ar_s = [cat(stack(at[rs, sl]), stack(rt[rs, sl])) for sl in sls]
        bk_s = [cat(stack(bt[rs, sl]), stack(kt[rs, sl])) for sl in sls]
        v_s = [stack(v[rs, sl]) for sl in sls]
        bkh_s = [cat(stack(bh[rs, sl]), stack(kh[rs, sl])) for sl in sls]
        if H % LANE == 0:
            g = [_dot_nt(ar_s[p], bk_s[p]) for p in pairs]
            g4 = [(x[0:H, 0:H], x[0:H, H:], x[H:, 0:H], x[H:, H:]) for x in g]
        else:
            g4 = [(_dot_nt(ar_s[p][0:H], bk_s[p][0:H]), _dot_nt(ar_s[p][0:H], bk_s[p][H:]),
                   _dot_nt(ar_s[p][H:], bk_s[p][0:H]), _dot_nt(ar_s[p][H:], bk_s[p][H:])) for p in pairs]
        a_ab = [jnp.where(strict, x[0], 0.0) for x in g4]
        akrk = [cat(jnp.where(strict, x[1], 0.0).astype(BF16), jnp.where(incl, x[3], 0.0).astype(BF16))
                for x in g4]
        a_rb = [jnp.where(incl, x[2], 0.0).astype(BF16) for x in g4]
        t_inv = [eye + x for x in a_ab]
        pw = [x.astype(BF16) for x in a_ab]
        pw = [_dot(x, x).astype(BF16) for x in pw]
        n = 2
        while n < L:
            last = 2 * n >= L
            if H % LANE == 0 and not last:
                pr = [_dot(pw[p], jnp.concatenate([pw[p], t_inv[p].astype(BF16)], axis=1)) for p in pairs]
                t_inv = [t_inv[p] + pr[p][:, H:] for p in pairs]
                pw = [x[:, 0:H].astype(BF16) for x in pr]
            else:
                t_inv = [t_inv[p] + _dot(pw[p], t_inv[p].astype(BF16)) for p in pairs]
                if not last:
                    pw = [_dot(x, x).astype(BF16) for x in pw]
            n *= 2
        return ar_s, v_s, bkh_s, akrk, a_rb, [x.astype(BF16) for x in t_inv]

    def advance(q, pre):
        ar_s, v_s, bkh_s, akrk, a_rb, t_inv = pre
        st_f = [state[p] for p in pairs]
        ars = [_dot_nt(ar_s[p], st_f[p].astype(BF16)) for p in pairs]
        av = [_dot(akrk[p], v_s[p]) for p in pairs]
        u_b = [_dot(t_inv[p], (ars[p][0:H] + av[p][0:H]).astype(BF16)).astype(BF16) for p in pairs]
        y_s = [ars[p][H:] + av[p][H:] + _dot(a_rb[p], u_b[p]) for p in pairs]
        for p in pairs:
            state[p] = st_f[p] * g_ls[q][:, sls[p]] + _dot_tn(cat(u_b[p], v_s[p]), bkh_s[p])
        return jnp.concatenate([x[0:L] + x[L:] for x in y_s], axis=1)

    pres = [prepare(q) for q in range(cpb)]
    ys = [advance(q, pres[q]) for q in range(cpb)]
    y = ys[0] if cpb == 1 else jnp.concatenate(ys, axis=0)

    bdm = bd * (1.0 / B_HEAD)
    mean = _seg_sum(y, bdm)
    yc = y - mean
    var = _seg_sum(yc * yc, bdm)
    ob_ref[...] = yc * lax.rsqrt(var + GN_EPS) * lw_ref[...] + lb_ref[...] + bonus

    @pl.when(c == steps_per_seq - 1)
    def _():
        for p in range(npair):
            sp = state[p]
            sout_ref[2 * p] = sp[0:B_HEAD, 0:B_HEAD]
            sout_ref[2 * p + 1] = sp[B_HEAD:, B_HEAD:]


def _scan(z, inj, s0, wts, *, n_seq, seq_len, chunk, cpb, n_valid):
    r = z.shape[0]
    d = wts[0].shape[1]
    npair = d // LANE
    rows = chunk * cpb
    cps = seq_len // rows
    zz_blk = (5 * d + 640) // LANE
    use_inj = inj is not None
    if inj is None:
        inj = jnp.zeros((n_seq, SUBLANE, d), F32)
    ia = jnp.arange(rows)
    tri = ((ia[:, None] >= ia[None, :]) & (ia[:, None] // chunk == ia[None, :] // chunk)).astype(BF16)
    bd = (jnp.arange(2 * LANE)[:, None] // B_HEAD == jnp.arange(2 * LANE)[None, :] // B_HEAD).astype(BF16)
    full = lambda a: pl.BlockSpec(a.shape, lambda b, c: (0,) * a.ndim)
    colblk = lambda j: pl.BlockSpec((rows,d), lambda b, c: (b * cps + c, j))
    consts = list(wts) + [tri, bd]
    return pl.pallas_call(
        functools.partial(_scan_kernel, steps_per_seq=cps, cpb=cpb, n_valid=n_valid, use_inj=use_inj),
        grid=(n_seq, cps),
        in_specs=[colblk(0), colblk(1), colblk(2),
                  pl.BlockSpec((rows,LANE), lambda b, c: (b * cps + c, zz_blk)),
                  pl.BlockSpec((None, SUBLANE, d), lambda b, c: (b, 0, 0)),
                  pl.BlockSpec((None,) + s0.shape[1:], lambda b, c: (b, 0, 0, 0))]
                 + [full(a) for a in consts],
        out_specs=[pl.BlockSpec((rows,d), lambda b, c: (b * cps + c, 0)),
                   pl.BlockSpec((None,) + s0.shape[1:], lambda b, c: (b, 0, 0, 0))],
        out_shape=[jax.ShapeDtypeStruct((r, d), F32), jax.ShapeDtypeStruct(s0.shape, F32)],
        scratch_shapes=[pltpu.VMEM((npair, LANE, LANE), F32), pltpu.VMEM((SUBLANE, d), F32)],
        compiler_params=_cp(("arbitrary", "arbitrary")),
        name="rwkv_scan",
    )(z, z, z, z, inj, s0, *consts)


def _merge_kernel(oa_ref, ob_ref, ga_ref, gb_ref, x_ref, g1_ref, wo_ref, *rest, sample):
    if sample:
        wuv_ref, o_ref = rest
        kvr = wuv_ref.shape[1]
        oa = jnp.concatenate(
            [_dot(oa_ref[:, h * kvr:(h + 1) * kvr].astype(BF16), wuv_ref[h]) for h in range(A_HEADS)], axis=1)
    else:
        (o_ref,) = rest
        oa = oa_ref[...]
    merged = _sigmoid(ga_ref[...]) * oa + _sigmoid(gb_ref[...]) * ob_ref[...]
    o_ref[...] = x_ref[...] + g1_ref[...] * _dot(merged.astype(BF16), wo_ref[...])


def _merge(oa, ob, z, x, g1, wo, wuv, rows_per_batch, tm):
    r, d = x.shape
    sample = wuv is not None
    full = lambda a: pl.BlockSpec(a.shape, lambda i: (0,) * a.ndim)
    row = lambda n: pl.BlockSpec((tm, n), lambda i: (i, 0))
    in_specs = [row(oa.shape[1]), row(d), pl.BlockSpec((tm, d), lambda i: (i, 3)),
                pl.BlockSpec((tm, d), lambda i: (i, 4)), row(d), _mod_spec(g1, tm, rows_per_batch), full(wo)]
    args = [oa, ob, z, z, x, g1, wo]
    if sample:
        in_specs.append(full(wuv))
        args.append(wuv)
    return pl.pallas_call(
        functools.partial(_merge_kernel, sample=sample),
        grid=(r // tm,),
        in_specs=in_specs, out_specs=row(d),
        out_shape=jax.ShapeDtypeStruct((r, d), F32),
        compiler_params=_cp(("arbitrary",)),
        name="merge_out",
    )(*args)


def _ffn_kernel(x_ref, sc_ref, sh_ref, g2_ref, gn_ref, wu_ref, wv_ref, cw_ref, cb_ref, wd_ref, *rest,
                seq_len, sample, final):
    rest = list(rest)
    if sample:
        inj1_ref, inj2_ref = rest[:2]
        rest = rest[2:]
    if final:
        gf_ref = rest.pop(0)
    x2_ref, u_ref = rest[:2]
    rest = rest[2:]
    if final:
        y_ref = rest.pop(0)
    h_ref, acc_ref, carry_ref = rest
    i, j = pl.program_id(0), pl.program_id(1)
    tm = x_ref.shape[0]

    @pl.when(j == 0)
    def _():
        y = _rms(x_ref[...], gn_ref[...])
        h_ref[...] = (y * (1.0 + sc_ref[...]) + sh_ref[...]).astype(BF16)
        acc_ref[...] = jnp.zeros_like(acc_ref)

    h = h_ref[...]
    u = _dot(h, wu_ref[...])
    val = _dot(h, wv_ref[...])
    row = lax.broadcasted_iota(jnp.int32, (tm, 1), 0)
    r1, r2 = pltpu.roll(u, 1, axis=0), pltpu.roll(u, 2, axis=0)
    if sample:
        t = row & (seq_len - 1)
        u1 = jnp.where(t == 0, inj1_ref[...], r1)
        u2 = jnp.where(t < 2, inj2_ref[...], r2)
        u_ref[...] = u
    else:
        @pl.when(((i * tm) & (seq_len - 1)) == 0)
        def _():
            carry_ref[j] = jnp.zeros(carry_ref.shape[1:], F32)

        prev = carry_ref[j]
        u1 = jnp.where(row == 0, prev[7:8, :], r1)
        u2 = jnp.where(row == 0, prev[6:7, :], jnp.where(row == 1, prev[7:8, :], r2))
        carry_ref[j] = u[tm - SUBLANE:tm, :]
        u_ref[...] = u[tm - SUBLANE:tm, :]
    conv = cb_ref[...] + cw_ref[0:1, :] * u2 + cw_ref[1:2, :] * u1 + cw_ref[2:3, :] * u
    act = (conv * _sigmoid(conv) * val).astype(BF16)
    acc_ref[...] += _dot(act, wd_ref[...])

    @pl.when(j == pl.num_programs(1) - 1)
    def _():
        x2 = x_ref[...] + g2_ref[...] * acc_ref[...]
        x2_ref[...] = x2
        if final:
            y_ref[...] = _rms(x2, gf_ref[...])


def _ffn(x, sc, sh, g2, gn, wu, wv, cw, cb, wd, inj, gf, *, rows_per_batch, seq_len, tm, tf):
    r, d = x.shape
    f = wu.shape[1]
    sample, final = inj is not None, gf is not None
    nf = f // tf
    cst = lambda a: pl.BlockSpec(a.shape, lambda i, j: (0,) * a.ndim)
    in_specs = [pl.BlockSpec((tm, d), lambda i, j: (i, 0)),
                _mod_spec(sc, tm, rows_per_batch), _mod_spec(sh, tm, rows_per_batch),
                _mod_spec(g2, tm, rows_per_batch), cst(gn),
                pl.BlockSpec((d, tf), lambda i, j: (0, j)), pl.BlockSpec((d, tf), lambda i, j: (0, j)),
                pl.BlockSpec((3, tf), lambda i, j: (0, j)), pl.BlockSpec((1, tf), lambda i, j: (0, j)),
                pl.BlockSpec((tf, d), lambda i, j: (j, 0))]
    args = [x, sc, sh, g2, gn, wu, wv, cw, cb, wd]
    if sample:
        in_specs += [pl.BlockSpec((tm, tf), lambda i, j: (i, j))] * 2
        args += list(inj)
    if final:
        in_specs.append(cst(gf))
        args.append(gf)
    out_specs = [pl.BlockSpec((tm, d), lambda i, j: (i, 0))]
    out_shape = [jax.ShapeDtypeStruct((r, d), F32)]
    if sample:
        out_specs.append(pl.BlockSpec((tm, tf), lambda i, j: (i, j)))
        out_shape.append(jax.ShapeDtypeStruct((r, f), F32))
    else:
        out_specs.append(pl.BlockSpec((None, SUBLANE, tf), lambda i, j: (i, 0, j)))
        out_shape.append(jax.ShapeDtypeStruct((r // tm, SUBLANE, f), F32))
    if final:
        out_specs.append(pl.BlockSpec((tm, d), lambda i, j: (i, 0)))
        out_shape.append(jax.ShapeDtypeStruct((r, d), F32))
    return pl.pallas_call(
        functools.partial(_ffn_kernel, seq_len=seq_len, sample=sample, final=final),
        grid=(r // tm, nf),
        in_specs=in_specs, out_specs=out_specs, out_shape=out_shape,
        scratch_shapes=[pltpu.VMEM((tm, d), BF16), pltpu.VMEM((tm, d), F32),
                        pltpu.VMEM((nf, SUBLANE, tf), F32)],
        compiler_params=_cp(("arbitrary", "arbitrary")),
        name="conv_ffn",
    )(*args)


def _pad_cols(a, n):
    return jnp.pad(a, ((0, 0), (0, n - a.shape[1])))


def _rot(a):
    half = a.shape[-1] // 2
    return jnp.concatenate([-a[..., half:], a[..., :half]], axis=-1)


def _layer_weights(l, w_in, mu_shift, w_uq, w_uk, w_uv, w2, a2, d, q_rank, kv_rank):
    dd = B_HEAD
    o_zb = q_rank + kv_rank + QK_ROPE
    o_r, o_zw, o_k, o_v, o_za = 0, d, d + dd, 2 * d + dd, 3 * d + dd
    n_zb = 3 * d + 2 * dd
    wi = w_in[l]
    zb = wi[:, o_zb:o_zb + n_zb]
    kpe_w = wi[:, q_rank + kv_rank:o_zb]
    misc = jnp.concatenate([wi[:, :q_rank + kv_rank], zb[:, o_zw:o_zw + dd], zb[:, o_za:o_za + dd],
                            _pad_cols(kpe_w, LANE), _pad_cols(_rot(kpe_w), LANE)], axis=1)
    w_z = jnp.concatenate([zb[:, o_r:o_r + d], zb[:, o_k:o_k + d], zb[:, o_v:o_v + d],
                           wi[:, o_zb + n_zb:o_zb + n_zb + 2 * d], misc], axis=1).astype(BF16)
    mu = mu_shift[l]
    mu_rkv = jnp.stack([mu[o_r:o_r + d], mu[o_k:o_k + d], mu[o_v:o_v + d]])
    mu_zz = jnp.concatenate([mu[o_zw:o_zw + dd], mu[o_za:o_za + dd]])[None]
    wq3 = w_uq[l].reshape(q_rank, A_HEADS, QK_NOPE + QK_ROPE)
    nope, rope = wq3[..., :QK_NOPE], wq3[..., QK_NOPE:]
    zpad = jnp.zeros((q_rank, A_HEADS, LANE - QK_NOPE - QK_ROPE), F32)
    wq = jnp.concatenate([rope, nope, zpad], axis=-1).reshape(q_rank, A_HEADS * LANE).astype(BF16)
    wqr = jnp.concatenate([_rot(rope), jnp.zeros_like(nope), zpad], axis=-1).reshape(q_rank, A_HEADS * LANE).astype(BF16)
    zk = jnp.zeros((kv_rank, A_HEADS, QK_ROPE), F32)
    wk = jnp.concatenate([zk, w_uk[l], zk], axis=-1).reshape(kv_rank, A_HEADS * LANE).astype(BF16)
    wv = w_uv[l].reshape(kv_rank, -1).T.astype(BF16)
    ukt = jnp.transpose(w_uk[l], (1, 2, 0))
    zr = jnp.zeros((A_HEADS, QK_ROPE, kv_rank), F32)
    wabs = jnp.concatenate([zr, ukt, zr], axis=1).astype(BF16)
    wuv_h = jnp.transpose(w_uv[l], (1, 0, 2)).astype(BF16)
    zrow = jnp.zeros((dd, d), F32)
    w2p = jnp.concatenate([w2[l], zrow], axis=0).astype(BF16)
    a2p = jnp.concatenate([zrow, a2[l]], axis=0).astype(BF16)
    return dict(w_z=w_z, mu_rkv=mu_rkv, mu_zz=mu_zz, wq=wq, wqr=wqr, wk=wk, wv=wv, wabs=wabs,
                wuv_h=wuv_h, w2p=w2p, a2p=a2p)


def _unperm_shift(zrow, d):
    dd = B_HEAD
    m = 5 * d
    zw, za = zrow[:, m + 640:m + 640 + dd], zrow[:, m + 640 + dd:m + 640 + 2 * dd]
    return jnp.concatenate([zrow[:, 0:d], zw, zrow[:, d:2 * d], zrow[:, 2 * d:3 * d], za], axis=1)


def _shift_inj(state_shift_l, d):
    dd = B_HEAD
    s = state_shift_l
    r, zw, k, v, za = s[:, 0:d], s[:, d:d + dd], s[:, d + dd:2 * d + dd], s[:, 2 * d + dd:3 * d + dd], s[:, 3 * d + dd:]
    zz = jnp.concatenate([zw, za, jnp.zeros((s.shape[0], d - 2 * dd), F32)], axis=1)
    rows = jnp.stack([r, k, v, zz], axis=1)
    return jnp.concatenate([rows, jnp.zeros((s.shape[0], SUBLANE - 4, d), F32)], axis=1)


def kernel(x_prompt, x_sample, c_prompt, c_sample, cache_ckv, cache_kpe, state_wkv, state_shift, state_conv, page_table, w_ada, b_ada, g_mix, g_ffn, w_in, g_q, w_uq, g_kv, w_uk, w_uv, mu_shift, w0, w2, a0, a2, k_k, k_a, r_k, lnx_w, lnx_b, w_out, w_up, conv_w, conv_b, w_down, g_final):
    bp, tp, d = x_prompt.shape
    bs, ts, _ = x_sample.shape
    depth = w_in.shape[0]
    q_rank, kv_rank = g_q.shape[1], g_kv.shape[1]
    d_ff = w_down.shape[1]
    n_pages, page = page_table.shape[1], cache_ckv.shape[2]
    past_len = n_pages * page
    heads_b = d // B_HEAD
    rp, rs = bp * tp, bs * S_PAD

    tm_p = min(512, tp)
    tm_s = min(256, rs)
    tq = min(1024, tp)
    kc, fl_sub, fl_unroll = min(128, tp), min(256, tq), 8
    chunk_p = min(64, tp)
    cpb_p = 4 if tp % (4 * chunk_p) == 0 else 1
    tf = d_ff // 2

    mod = _ada(jnp.concatenate([c_prompt, c_sample], axis=0), w_ada, b_ada)
    invf = ROPE_THETA ** (-jnp.arange(0, QK_ROPE, 2, dtype=F32) / QK_ROPE)
    invf = jnp.tile(invf, LANE // invf.shape[0])[None]

    cache_kpe_t = jnp.swapaxes(cache_kpe, 2, 3)
    xp = x_prompt.reshape(rp, d)
    xs = jnp.pad(x_sample, ((0, 0), (0, S_PAD - ts), (0, 0))).reshape(rs, d)
    new_p = [[] for _ in range(5)]
    new_s = [[] for _ in range(5)]
    yp = ys = None
    for l in range(depth):
        w = _layer_weights(l, w_in, mu_shift, w_uq, w_uk, w_uv, w2, a2, d, q_rank, kv_rank)
        mods_p = [m[:, None, :] for m in jnp.split(mod[l, :bp], 6, axis=-1)]
        mods_s = [jnp.repeat(m, S_PAD, axis=0) for m in jnp.split(mod[l, bp:], 6, axis=-1)]
        row1 = lambda a: a[l][None]
        scan_w = [w['mu_rkv'], w['mu_zz'], row1(w0), w['w2p'], row1(a0), w['a2p'], row1(k_k), row1(k_a),
                  r_k[l].reshape(1, d), row1(lnx_w), row1(lnx_b)]
        wu, wv_up = w_up[l][:, :d_ff].astype(BF16), w_up[l][:, d_ff:].astype(BF16)
        wd, wo = w_down[l].astype(BF16), w_out[l].astype(BF16)
        gf = g_final[None] if l == depth - 1 else None

        sh1, sc1, g1, sh2, sc2, g2 = mods_p
        z = _normproj(xp, sc1, sh1, row1(g_mix), w['w_z'], tp, min(1024, tp), 1536)
        ckv, kpe, q, k, vt = _prep(z, row1(g_q), w['wq'], w['wqr'], row1(g_kv), invf, [w['wk'], w['wv']],
                                   seq_len=tp, pos_base=0, sample=False, tm=tm_p, kv_chunk=kc)
        oa = _flash(q.reshape(bp, tp, d), k.reshape(bp, tp, d), vt, tq, kc, fl_sub, fl_unroll).reshape(rp, d)
        ob, sp = _scan(z, None, jnp.zeros((bp, heads_b, B_HEAD, B_HEAD), F32), scan_w,
                       n_seq=bp, seq_len=tp, chunk=chunk_p, cpb=cpb_p, n_valid=chunk_p)
        x1 = _merge(oa, ob, z, xp, g1, wo, None, tp, tm_p)
        outs = _ffn(x1, sc2, sh2, g2, row1(g_ffn), wu, wv_up, conv_w[l], row1(conv_b), wd, None, gf,
                    rows_per_batch=tp, seq_len=tp, tm=tm_p, tf=tf)
        xp, utail = outs[0], outs[1]
        if gf is not None:
            yp = outs[2]
        zlast = z.reshape(bp, tp, -1)[:, tp - 1]
        new_p[0].append(ckv.reshape(bp, tp, kv_rank))
        new_p[1].append(kpe.reshape(bp, tp, QK_ROPE))
        new_p[2].append(sp)
        new_p[3].append(_unperm_shift(zlast, d))
        new_p[4].append(utail.reshape(bp, tp // tm_p, SUBLANE, d_ff)[:, -1, SUBLANE - 2:])

        sh1, sc1, g1, sh2, sc2, g2 = mods_s
        z = _normproj(xs, sc1, sh1, row1(g_mix), w['w_z'], rs, tm_s, 1536)
        ckv, kpe, qlat, qpe = _prep(z, row1(g_q), w['wq'], w['wqr'], row1(g_kv), invf, [w['wabs']],
                                    seq_len=S_PAD, pos_base=past_len, sample=True, tm=tm_s)
        qlat = qlat.reshape(bs, S_PAD, A_HEADS, kv_rank)[:, :ts].reshape(bs, ts * A_HEADS, kv_rank)
        qpe = qpe.reshape(bs, S_PAD, A_HEADS, LANE)[:, :ts, :, :QK_ROPE].reshape(bs, ts * A_HEADS, QK_ROPE)
        ckv3, kpe3 = ckv.reshape(bs, S_PAD, kv_rank), kpe.reshape(bs, S_PAD, QK_ROPE)
        olat = _sattn(page_table, qlat, qpe, ckv3, kpe3, cache_ckv, cache_kpe_t, l, ts)
        olat = jnp.pad(olat.reshape(bs, ts, A_HEADS * kv_rank), ((0, 0), (0, S_PAD - ts), (0, 0)))
        ob, sp = _scan(z, _shift_inj(state_shift[l], d), state_wkv[l], scan_w,
                       n_seq=bs, seq_len=S_PAD, chunk=S_PAD, cpb=1, n_valid=ts)
        x1 = _merge(olat.reshape(rs, -1), ob, z, xs, g1, wo, w['wuv_h'], rs, tm_s)
        cp = state_conv[l]
        zc = jnp.zeros((bs, S_PAD, d_ff), F32)
        inj1 = zc.at[:, 0].set(cp[:, 1]).reshape(rs, d_ff)
        inj2 = zc.at[:, 0].set(cp[:, 0]).at[:, 1].set(cp[:, 1]).reshape(rs, d_ff)
        outs = _ffn(x1, sc2, sh2, g2, row1(g_ffn), wu, wv_up, conv_w[l], row1(conv_b), wd, (inj1, inj2), gf,
                    rows_per_batch=rs, seq_len=S_PAD, tm=tm_s, tf=tf)
        xs, u_full = outs[0], outs[1]
        if gf is not None:
            ys = outs[2]
        new_s[0].append(ckv3[:, :ts])
        new_s[1].append(kpe3[:, :ts])
        new_s[2].append(sp)
        new_s[3].append(_unperm_shift(z.reshape(bs, S_PAD, -1)[:, ts - 1], d))
        new_s[4].append(u_full.reshape(bs, S_PAD, d_ff)[:, ts - 2:ts])

    stack = lambda u: jnp.stack(u, axis=0)
    return (yp.reshape(bp, tp, d), ys.reshape(bs, S_PAD, d)[:, :ts],
            stack(new_p[0]), stack(new_p[1]), stack(new_p[2]), stack(new_p[3]), stack(new_p[4]),
            stack(new_s[0]), stack(new_s[1]), stack(new_s[2]), stack(new_s[3]), stack(new_s[4]))
```

```python
import functools

import jax
import jax.numpy as jnp
from jax import lax
from jax.experimental import pallas as pl
from jax.experimental.pallas import tpu as pltpu

F32, BF16 = jnp.float32, jnp.bfloat16

EPS = 1e-6
GN_EPS = 64e-5
ROPE_THETA = 10000.0
A_HEADS = 8
QK_NOPE = 64
QK_ROPE = 32
B_HEAD = 64
ATTN_SCALE = (QK_NOPE + QK_ROPE) ** -0.5
LOG2E = 1.4426950408889634

LANE = 128
SUBLANE = 8
V7X_VMEM_LIMIT = 56 * 1024 * 1024
S_PAD = 8
FFN_SUB = 256


def _cp(sem):
    return pltpu.CompilerParams(dimension_semantics=sem, vmem_limit_bytes=V7X_VMEM_LIMIT)


def _dot(a, b):
    return jnp.dot(a, b, preferred_element_type=F32)


def _dot_nt(a, b):
    return lax.dot_general(a, b, (((1,), (1,)), ((), ())), preferred_element_type=F32)


def _dot_tn(a, b):
    return lax.dot_general(a, b, (((0,), (0,)), ((), ())), preferred_element_type=F32)


def _split2(x):
    hi = x.astype(BF16)
    return hi, (x - hi.astype(F32)).astype(BF16)


def _split3(x):
    hi = x.astype(BF16)
    r = x - hi.astype(F32)
    mid = r.astype(BF16)
    return hi, mid, (r - mid.astype(F32)).astype(BF16)


def _dot3(a, b):
    ah, al = _split2(a)
    bh, bl = _split2(b)
    return _dot(ah, bh) + (_dot(ah, bl) + _dot(al, bh))


def _sigmoid(x):
    return 1.0 / (1.0 + jnp.exp(-x))


def _rms(x, g):
    return x * lax.rsqrt(jnp.mean(x * x, axis=-1, keepdims=True) + EPS) * g


def _mod_spec(arr, tm, rows_per_batch):
    d = arr.shape[-1]
    if arr.ndim == 3:
        bpb = rows_per_batch // tm
        return pl.BlockSpec((None, 1, d), lambda i, *_: (i // bpb, 0, 0))
    return pl.BlockSpec((tm, d), lambda i, *_: (i, 0))


def _ada_kernel(c_ref, w_ref, b_ref, o_ref):
    c = c_ref[...]
    s = (c * _sigmoid(c)).astype(BF16)
    o_ref[...] = _dot(s, w_ref[...].astype(BF16)) + b_ref[...]


def _ada(c_all, w_ada, b_ada, tn=1536):
    depth, d, n = w_ada.shape
    bc = c_all.shape[0]
    return pl.pallas_call(
        _ada_kernel,
        grid=(depth, n // tn),
        in_specs=[pl.BlockSpec((bc, d), lambda l, j: (0, 0)),
                  pl.BlockSpec((None, d, tn), lambda l, j: (l, 0, j)),
                  pl.BlockSpec((None, 1, tn), lambda l, j: (l, 0, j))],
        out_specs=pl.BlockSpec((None, bc, tn), lambda l, j: (l, 0, j)),
        out_shape=jax.ShapeDtypeStruct((depth, bc, n), F32),
        compiler_params=_cp(("arbitrary", "arbitrary")),
        name="ada",
    )(c_all, w_ada, b_ada.reshape(depth, 1, n))


def _normproj_kernel(x_ref, sc_ref, sh_ref, g_ref, w_ref, o_ref, h_ref):
    @pl.when(pl.program_id(1) == 0)
    def _():
        y = _rms(x_ref[...], g_ref[...])
        h_ref[...] = (y * (1.0 + sc_ref[...]) + sh_ref[...]).astype(BF16)

    o_ref[...] = _dot(h_ref[...], w_ref[...])


def _normproj(x, sc, sh, g, w, rows_per_batch, tm, tn):
    r, d = x.shape
    n = w.shape[1]
    return pl.pallas_call(
        _normproj_kernel,
        grid=(r // tm, n // tn),
        in_specs=[pl.BlockSpec((tm, d), lambda i, j: (i, 0)),
                  _mod_spec(sc, tm, rows_per_batch), _mod_spec(sh, tm, rows_per_batch),
                  pl.BlockSpec((1, d), lambda i, j: (0, 0)),
                  pl.BlockSpec((d, tn), lambda i, j: (0, j))],
        out_specs=pl.BlockSpec((tm, tn), lambda i, j: (i, j)),
        out_shape=jax.ShapeDtypeStruct((r, n), F32),
        scratch_shapes=[pltpu.VMEM((tm, d), BF16)],
        compiler_params=_cp(("arbitrary", "arbitrary")),
        name="normproj",
    )(x, sc, sh, g, w)


def _prep_kernel(misc_ref, gq_ref, wq_ref, wqr_ref, gkv_ref, invf_ref, *rest,
                 seq_len, pos_base, q_rank, kv_rank, sample):
    tm = misc_ref.shape[0]
    i = pl.program_id(0)
    row = lax.broadcasted_iota(jnp.int32, (tm, LANE), 0) + i * tm
    pos = (pos_base + (row & (seq_len - 1))).astype(F32)
    ang = pos * invf_ref[...]
    cosv, sinv = jnp.cos(ang), jnp.sin(ang)
    lane = lax.broadcasted_iota(jnp.int32, (tm, LANE), 1)
    is_pe = lane < QK_ROPE
    ck = jnp.where(is_pe, cosv, 0.0)
    sk = jnp.where(is_pe, sinv, 0.0)
    qscale = ATTN_SCALE if sample else ATTN_SCALE * LOG2E
    cq = jnp.where(is_pe, cosv, jnp.where(lane < QK_ROPE + QK_NOPE, 1.0, 0.0)) * qscale
    sq = sk * qscale

    qn = _rms(misc_ref[:, 0:q_rank], gq_ref[...]).astype(BF16)
    q = _dot(qn, wq_ref[...])
    qr = _dot(qn, wqr_ref[...])
    ckv = _rms(misc_ref[:, q_rank:q_rank + kv_rank], gkv_ref[...])
    ckv_b = ckv.astype(BF16)
    o_kpe = q_rank + kv_rank + LANE
    kpe = misc_ref[:, o_kpe:o_kpe + LANE] * ck + misc_ref[:, o_kpe + LANE:o_kpe + 2 * LANE] * sk

    if sample:
        wabs_ref, ckv_ref, kpe_ref, qlat_ref, qpe_ref = rest
        for h in range(A_HEADS):
            sl = slice(h * LANE, (h + 1) * LANE)
            qh = (q[:, sl] * cq + qr[:, sl] * sq).astype(BF16)
            qpe_ref[:, sl] = qh
            qlat_ref[:, h * kv_rank:(h + 1) * kv_rank] = _dot(qh, wabs_ref[h]).astype(BF16)
    else:
        wk_ref, wvt_ref, ckv_ref, kpe_ref, q_ref, k_ref, vt_ref = rest
        kn = _dot(ckv_b, wk_ref[...])
        for h in range(A_HEADS):
            sl = slice(h * LANE, (h + 1) * LANE)
            q_ref[:, sl] = (q[:, sl] * cq + qr[:, sl] * sq).astype(BF16)
            k_ref[:, sl] = (kn[:, sl] + kpe).astype(BF16)
        vt = _dot_nt(wvt_ref[...], ckv_b)
        kc = vt_ref.shape[2]
        for c in range(tm // kc):
            vt_ref[c] = vt[:, c * kc:(c + 1) * kc].astype(BF16)
    ckv_ref[...] = ckv
    kpe_ref[...] = kpe[:, 0:QK_ROPE]


def _prep(z, gq, wq, wqr, gkv, invf, extra, *, seq_len, pos_base, sample, tm, kv_chunk=None):
    r = z.shape[0]
    q_rank, kv_rank = gq.shape[1], gkv.shape[1]
    d = wq.shape[1]
    misc_blk = z.shape[1] // d - 1
    full = lambda a: pl.BlockSpec(a.shape, lambda i: (0,) * a.ndim)
    row = lambda n: pl.BlockSpec((tm, n), lambda i: (i, 0))
    in_specs = [pl.BlockSpec((tm, d), lambda i: (i, misc_blk)), full(gq), full(wq), full(wqr),
                full(gkv), full(invf)] + [full(a) for a in extra]
    out_specs = [row(kv_rank), row(QK_ROPE)]
    out_shape = [jax.ShapeDtypeStruct((r, kv_rank), F32), jax.ShapeDtypeStruct((r, QK_ROPE), F32)]
    if sample:
        out_specs += [row(A_HEADS * kv_rank), row(d)]
        out_shape += [jax.ShapeDtypeStruct((r, A_HEADS * kv_rank), BF16), jax.ShapeDtypeStruct((r, d), BF16)]
    else:
        out_specs += [row(d), row(d), pl.BlockSpec((tm // kv_chunk, d, kv_chunk), lambda i: (i, 0, 0))]
        out_shape += [jax.ShapeDtypeStruct((r, d), BF16)] * 2 + [jax.ShapeDtypeStruct((r // kv_chunk, d, kv_chunk), BF16)]
    return pl.pallas_call(
        functools.partial(_prep_kernel, seq_len=seq_len, pos_base=pos_base, q_rank=q_rank,
                          kv_rank=kv_rank, sample=sample),
        grid=(r // tm,),
        in_specs=in_specs, out_specs=out_specs, out_shape=out_shape,
        compiler_params=_cp(("arbitrary",)),
        name="mla_prep_sample" if sample else "mla_prep_prompt",
    )(z, gq, wq, wqr, gkv, invf, *extra)


def _flash_kernel(q_ref, k_ref, vt_ref, o_ref, *, tq, kc, sub, unroll, pv_depth):
    qi = pl.program_id(2)
    nsub = tq // sub
    qs = [q_ref[s * sub:(s + 1) * sub, :] for s in range(nsub)]

    def group(j0, tiles, carries):
        carries = list(carries)
        chunks = sorted({c for c, _, _ in tiles})
        ks = {c: k_ref[pl.ds(pl.multiple_of((j0 + c) * kc, kc), kc), :] for c in chunks}
        vts = {c: vt_ref[j0 + c] for c in chunks}
        sts = [_dot_nt(ks[c], qs[s]) for c, s, _ in tiles]
        sts = [st if mask is None else jnp.where(mask, st, -jnp.inf) for st, (_, _, mask) in zip(sts, tiles)]
        mloc = [jnp.max(st, axis=0, keepdims=True) for st in sts]
        units = []
        for s in range(nsub):
            idx = [i for i, (_, s_, _) in enumerate(tiles) if s_ == s]
            units += [(s, idx[i:i + pv_depth]) for i in range(0, len(idx), pv_depth)]
        ps, alphas = [], []
        for s, idx in units:
            m, l, acc = carries[s]
            m_new = m
            for i in idx:
                m_new = jnp.maximum(m_new, mloc[i])
            alpha = jnp.exp2(m - m_new)
            p = [jnp.exp2(sts[i] - m_new) for i in idx]
            l = alpha * l
            for x in p:
                l = l + jnp.sum(x, axis=0, keepdims=True)
            carries[s] = (m_new, l, acc)
            ps.append(jnp.concatenate([x.astype(BF16) for x in p], axis=0))
            alphas.append(alpha)
        pvs = [_dot(jnp.concatenate([vts[tiles[i][0]] for i in idx], axis=1), p)
               for (_, idx), p in zip(units, ps)]
        for (s, _), alpha, pv in zip(units, alphas, pvs):
            m, l, acc = carries[s]
            carries[s] = (m, l, alpha * acc + pv)
        return tuple(carries)

    def body(j, carries):
        return group(j * unroll, [(c, s, None) for c in range(unroll) for s in range(nsub)], carries)

    init = tuple((jnp.full((1, sub), -jnp.inf, F32), jnp.zeros((1, sub), F32),
                  jnp.zeros((vt_ref.shape[1], sub), F32)) for _ in range(nsub))
    per_q = tq // kc
    carries = lax.fori_loop(0, qi * (per_q // unroll), body, init)
    key = lax.broadcasted_iota(jnp.int32, (kc, sub), 0)
    qry = lax.broadcasted_iota(jnp.int32, (kc, sub), 1)
    for g0 in range(0, per_q, unroll):
        tiles = []
        for jj in range(g0, g0 + unroll):
            for s in range(nsub):
                if jj * kc > s * sub + sub - 1:
                    continue
                needs_mask = jj * kc + kc - 1 > s * sub
                tiles.append((jj, s, (key + jj * kc <= qry + s * sub) if needs_mask else None))
        carries = group(qi * per_q, tiles, carries)
    for s in range(nsub):
        _, l, acc = carries[s]
        o_ref[s * sub:(s + 1) * sub, :] = jnp.transpose(acc / l)


def _flash(q, k, vt, tq, kc, sub, unroll):
    b, t, d = q.shape
    return pl.pallas_call(
        functools.partial(_flash_kernel, tq=tq, kc=kc, sub=sub, unroll=unroll, pv_depth=1),
        grid=(b, A_HEADS, t // tq),
        in_specs=[pl.BlockSpec((None, tq, LANE), lambda b_, h, i: (b_, i, h)),
                  pl.BlockSpec((None, t, LANE), lambda b_, h, i: (b_, 0, h)),
                  pl.BlockSpec((t // kc, LANE, kc), lambda b_, h, i: (b_, h, 0))],
        out_specs=pl.BlockSpec((None, tq, LANE), lambda b_, h, i: (b_, i, h)),
        out_shape=jax.ShapeDtypeStruct((b, t, d), F32),
        compiler_params=_cp(("arbitrary", "arbitrary", "arbitrary")),
        name="flash_prompt",
    )(q, k, vt)


def _sattn_kernel(pt_ref, qlat_ref, qpe_ref, knew_ref, pnew_ref, ckv_hbm, kpe_hbm, o_ref,
                  kbuf, pbuf, sem, *, layer, n_pages, page, n_valid, ck):
    b = pl.program_id(0)
    nb = pl.num_programs(0)
    slot = b % 2
    spb = qlat_ref.shape[0]

    def copies(step, s):
        out = []
        for i in range(spb):
            for p in range(n_pages):
                pg = pt_ref[(step * spb + i) * n_pages + p]
                out.append(pltpu.make_async_copy(ckv_hbm.at[layer, pg], kbuf.at[s, i, pl.ds(p * page, page)],
                                                 sem.at[0, s]))
                out.append(pltpu.make_async_copy(kpe_hbm.at[layer, pg], pbuf.at[s, i, p], sem.at[1, s]))
        return out

    @pl.when(b == 0)
    def _():
        for c in copies(0, 0):
            c.start()

    @pl.when(b + 1 < nb)
    def _():
        for c in copies(b + 1, 1 - slot):
            c.start()

    for c in copies(b, slot):
        c.wait()

    ppt = ck // page
    seqs, tiles = range(spb), range(n_pages // ppt)
    qlat = [qlat_ref[i] for i in seqs]
    qpe = [qpe_ref[i] for i in seqs]
    kcs = [[kbuf[slot, i, t * ck:(t + 1) * ck, :].astype(BF16) for t in tiles] for i in seqs]
    s_pe = [[jnp.concatenate([_dot(qpe[i], pbuf[slot, i, p].astype(BF16))
                              for p in range(t * ppt, (t + 1) * ppt)], axis=1) for t in tiles] for i in seqs]
    ss = [[_dot_nt(qlat[i], kcs[i][t]) + s_pe[i][t] for t in tiles] for i in seqs]
    kn = [knew_ref[i].astype(BF16) for i in seqs]
    pn = [pnew_ref[i].astype(BF16) for i in seqs]
    s_new = [_dot_nt(qlat[i], kn[i]) + _dot_nt(qpe[i], pn[i]) for i in seqs]
    tok = jnp.right_shift(lax.broadcasted_iota(jnp.int32, s_new[0].shape, 0), A_HEADS.bit_length() - 1)
    key = lax.broadcasted_iota(jnp.int32, s_new[0].shape, 1)
    s_new = [jnp.where((key <= tok) & (key < n_valid), x, -jnp.inf) for x in s_new]
    for i in seqs:
        m = jnp.max(s_new[i], axis=-1, keepdims=True)
        for s in ss[i]:
            m = jnp.maximum(m, jnp.max(s, axis=-1, keepdims=True))
        p_new = jnp.exp(s_new[i] - m)
        ps = [jnp.exp(s - m) for s in ss[i]]
        l = jnp.sum(p_new, axis=-1, keepdims=True)
        for p in ps:
            l = l + jnp.sum(p, axis=-1, keepdims=True)
        acc = _dot(p_new.astype(BF16), kn[i])
        for t in tiles:
            acc = acc + _dot(ps[t].astype(BF16), kcs[i][t])
        o_ref[i] = acc / l


def _sattn(page_table, qlat, qpe, knew, pnew, cache_ckv, cache_kpe_t, layer, n_valid):
    bs, nq, kvr = qlat.shape
    n_pages = page_table.shape[1]
    page = cache_ckv.shape[2]
    past = n_pages * page
    ck = min(512, past)
    spb = 2 if bs % 2 == 0 else 1
    blk = lambda a: pl.BlockSpec((spb,) + a.shape[1:], lambda b, pt: (b,) + (0,) * (a.ndim - 1))
    return pl.pallas_call(
        functools.partial(_sattn_kernel, layer=layer, n_pages=n_pages, page=page, n_valid=n_valid, ck=ck),
        grid_spec=pltpu.PrefetchScalarGridSpec(
            num_scalar_prefetch=1, grid=(bs // spb,),
            in_specs=[blk(qlat), blk(qpe), blk(knew), blk(pnew),
                      pl.BlockSpec(memory_space=pl.ANY), pl.BlockSpec(memory_space=pl.ANY)],
            out_specs=pl.BlockSpec((spb, nq, kvr), lambda b, pt: (b, 0, 0)),
            scratch_shapes=[pltpu.VMEM((2, spb, past, kvr), F32),
                            pltpu.VMEM((2, spb, n_pages) + cache_kpe_t.shape[2:], F32),
                            pltpu.SemaphoreType.DMA((2, 2))]),
        out_shape=jax.ShapeDtypeStruct((bs, nq, kvr), F32),
        compiler_params=_cp(("arbitrary",)),
        name="paged_attn_sample",
    )(page_table.reshape(-1), qlat, qpe, knew, pnew, cache_ckv, cache_kpe_t)


def _seg_sum(x, bd):
    rows, w = x.shape[0], bd.shape[0]
    n = x.shape[1] // w
    hi = x.astype(BF16).astype(F32)
    lo = x - hi
    lhs = jnp.concatenate([part[:, c * w:(c + 1) * w] for part in (hi, lo) for c in range(n)], axis=0)
    out = _dot(lhs.astype(BF16), bd)
    return jnp.concatenate([out[c * rows:(c + 1) * rows] + out[(n + c) * rows:(n + c + 1) * rows]
                            for c in range(n)], axis=1)


def _scan_kernel(r_ref, k_ref, v_ref, zz_ref, inj_ref, s0_ref,
                 mu_ref, muz_ref, w0_ref, w2_ref, a0_ref, a2_ref, kk_ref, ka_ref, rk_ref,
                 lw_ref, lb_ref, tri_ref, bd_ref,
                 ob_ref, sout_ref,
                 state, carry, *, steps_per_seq, cpb, n_valid, use_inj):
    c = pl.program_id(1)
    rows, d = r_ref.shape
    L = rows // cpb
    nsq = s0_ref.shape[0]
    npair = d // LANE
    row = lax.broadcasted_iota(jnp.int32, (rows, 1), 0)
    first = c == 0

    @pl.when(first)
    def _():
        zero = jnp.zeros((B_HEAD, B_HEAD), F32)
        for sq in range(nsq):
            for p in range(npair):
                state[sq * npair + p] = jnp.concatenate(
                    [jnp.concatenate([s0_ref[sq, 2 * p], zero], axis=1),
                     jnp.concatenate([zero, s0_ref[sq, 2 * p + 1]], axis=1)], axis=0)
        carry[...] = jnp.zeros_like(carry)

    def shift(x, idx, mu):
        w = x.shape[1]
        if use_inj:
            prev_rows = [inj_ref[sq, idx:idx + 1, 0:w] for sq in range(nsq)]
            prev_row = prev_rows[0] if nsq == 1 else jnp.concatenate(
                [jnp.broadcast_to(x_, (L, w)) for x_ in prev_rows], axis=0)
            prev = jnp.where((row & (L - 1)) == 0, prev_row, pltpu.roll(x, 1, axis=0))
        else:
            prev = jnp.where(row == 0, carry[idx:idx + 1, 0:w], pltpu.roll(x, 1, axis=0))
        if not use_inj:
            carry[idx:idx + 1, 0:x.shape[1]] = x[rows - 1:rows, :]
        return x + (prev - x) * mu

    r = shift(r_ref[...], 0, mu_ref[0:1, :])
    k = shift(k_ref[...], 1, mu_ref[1:2, :])
    v = shift(v_ref[...], 2, mu_ref[2:3, :])
    zz = shift(zz_ref[...], 3, muz_ref[...])

    wl = w0_ref[...] + _dot(jnp.tanh(zz).astype(BF16), w2_ref[...])
    sp = jnp.maximum(-wl, 0.0) + jnp.log(1.0 + jnp.exp(-jnp.abs(wl)))
    logw = -jnp.exp(-sp - 0.5)
    a_c = _sigmoid(a0_ref[...] + _dot(zz.astype(BF16), a2_ref[...]))
    kkr = k * kk_ref[...]
    bd = bd_ref[...]
    kk = kkr * lax.rsqrt(jnp.maximum(_seg_sum(kkr * kkr, bd), 1e-24))
    k = k * (1.0 + (a_c - 1.0) * ka_ref[...])
    if n_valid < L:
        valid = (row & (L - 1)) < n_valid
        logw = jnp.where(valid, logw, 0.0)
        kk = jnp.where(valid, kk, 0.0)
        k = jnp.where(valid, k, 0.0)
        v = jnp.where(valid, v, 0.0)
    bonus = _seg_sum(r * k * rk_ref[...], bd) * v

    l1, l2, l3 = _split3(logw)
    tri = tri_ref[...]
    cum = _dot(tri, l1) + (_dot(tri, l2) + _dot(tri, l3))
    cum_ends = [cum[(q + 1) * L - 1:(q + 1) * L, :] for q in range(cpb)]
    cum_end = cum_ends[0] if cpb == 1 else jnp.concatenate(
        [jnp.broadcast_to(x, (L, d)) for x in cum_ends], axis=0)
    e_fwd = jnp.exp(cum)
    e_inv = jnp.exp(-cum)
    e_end = jnp.exp(cum_end - cum)
    rt = r * e_fwd
    at = -kk * jnp.exp(cum - logw)
    b = kk * a_c
    bt, kt = b * e_inv, k * e_inv
    bh, kh = b * e_end, k * e_end
    g_ls = [jnp.exp(x) for x in cum_ends]

    lane = lax.broadcasted_iota(jnp.int32, (1, LANE), 1)
    m_a = lane < B_HEAD
    tr = lax.broadcasted_iota(jnp.int32, (2 * L, 2 * L), 0)
    tc = lax.broadcasted_iota(jnp.int32, (2 * L, 2 * L), 1)
    strict, incl, eye = tr > tc, tr >= tc, (tr == tc).astype(F32)

    def stack(x):
        return jnp.concatenate([jnp.where(m_a, x, 0.0), jnp.where(m_a, 0.0, x)], axis=0).astype(BF16)

    pairs = range(npair)
    sls = [slice(p * LANE, (p + 1) * LANE) for p in pairs]
    cat = lambda x, y_: jnp.concatenate([x, y_], axis=0)
    H = 2 * L

    def prepare(q):
        rs = slice(q * L, (q + 1) * L)
        ar_s = [cat(stack(at[rs, sl]), stack(rt[rs, sl])) for sl in sls]
        bk_s = [cat(stack(bt[rs, sl]), stack(kt[rs, sl])) for sl in sls]
        v_s = [stack(v[rs, sl]) for sl in sls]
        bkh_s = [cat(stack(bh[rs, sl]), stack(kh[rs, sl])) for sl in sls]
        if H % LANE == 0:
            g = [_dot_nt(ar_s[p], bk_s[p]) for p in pairs]
            g4 = [(x[0:H, 0:H], x[0:H, H:], x[H:, 0:H], x[H:, H:]) for x in g]
        else:
            g4 = [(_dot_nt(ar_s[p][0:H], bk_s[p][0:H]), _dot_nt(ar_s[p][0:H], bk_s[p][H:]),
                   _dot_nt(ar_s[p][H:], bk_s[p][0:H]), _dot_nt(ar_s[p][H:], bk_s[p][H:])) for p in pairs]
        a_ab = [jnp.where(strict, x[0], 0.0) for x in g4]
        akrk = [cat(jnp.where(strict, x[1], 0.0).astype(BF16), jnp.where(incl, x[3], 0.0).astype(BF16))
                for x in g4]
        a_rb = [jnp.where(incl, x[2], 0.0).astype(BF16) for x in g4]
        t_inv = [eye + x for x in a_ab]
        pw = [x.astype(BF16) for x in a_ab]
        pw = [_dot(x, x).astype(BF16) for x in pw]
        n = 2
        while n < L:
            last = 2 * n >= L
            if H % LANE == 0 and not last:
                pr = [_dot(pw[p], jnp.concatenate([pw[p], t_inv[p].astype(BF16)], axis=1)) for p in pairs]
                t_inv = [t_inv[p] + pr[p][:, H:] for p in pairs]
                pw = [x[:, 0:H].astype(BF16) for x in pr]
            else:
                t_inv = [t_inv[p] + _dot(pw[p], t_inv[p].astype(BF16)) for p in pairs]
                if not last:
                    pw = [_dot(x, x).astype(BF16) for x in pw]
            n *= 2
        return ar_s, v_s, bkh_s, akrk, a_rb, [x.astype(BF16) for x in t_inv]

    def advance(q, pre):
        ar_s, v_s, bkh_s, akrk, a_rb, t_inv = pre
        s_at = (q if nsq > 1 else 0) * npair
        st_f = [state[s_at + p] for p in pairs]
        ars = [_dot_nt(ar_s[p], st_f[p].astype(BF16)) for p in pairs]
        av = [_dot(akrk[p], v_s[p]) for p in pairs]
        u_b = [_dot(t_inv[p], (ars[p][0:H] + av[p][0:H]).astype(BF16)).astype(BF16) for p in pairs]
        y_s = [ars[p][H:] + av[p][H:] + _dot(a_rb[p], u_b[p]) for p in pairs]
        for p in pairs:
            state[s_at + p] = st_f[p] * g_ls[q][:, sls[p]] + _dot_tn(cat(u_b[p], v_s[p]), bkh_s[p])
        return jnp.concatenate([x[0:L] + x[L:] for x in y_s], axis=1)

    pres = [prepare(q) for q in range(cpb)]
    ys = [advance(q, pres[q]) for q in range(cpb)]
    y = ys[0] if cpb == 1 else jnp.concatenate(ys, axis=0)

    bdm = bd * (1.0 / B_HEAD)
    mean = _seg_sum(y, bdm)
    yc = y - mean
    var = _seg_sum(yc * yc, bdm)
    ob_ref[...] = yc * lax.rsqrt(var + GN_EPS) * lw_ref[...] + lb_ref[...] + bonus

    @pl.when(c == steps_per_seq - 1)
    def _():
        for sq in range(nsq):
            for p in range(npair):
                sp = state[sq * npair + p]
                sout_ref[sq, 2 * p] = sp[0:B_HEAD, 0:B_HEAD]
                sout_ref[sq, 2 * p + 1] = sp[B_HEAD:, B_HEAD:]


def _scan(z, inj, s0, wts, *, n_seq, seq_len, chunk, cpb, n_valid):
    r = z.shape[0]
    d = wts[0].shape[1]
    npair = d // LANE
    rows = chunk * cpb
    use_inj = inj is not None
    if use_inj:
        assert seq_len == chunk and n_seq % cpb == 0
        n_steps, cps, nsq = n_seq // cpb, 1, cpb
    else:
        n_steps, cps, nsq = n_seq, seq_len // rows, 1
        inj = jnp.zeros((n_seq, SUBLANE, d), F32)
    zz_blk = (5 * d + 640) // LANE
    ia = jnp.arange(rows)
    tri = ((ia[:, None] >= ia[None, :]) & (ia[:, None] // chunk == ia[None, :] // chunk)).astype(BF16)
    bd = (jnp.arange(2 * LANE)[:, None] // B_HEAD == jnp.arange(2 * LANE)[None, :] // B_HEAD).astype(BF16)
    full = lambda a: pl.BlockSpec(a.shape, lambda b, c: (0,) * a.ndim)
    colblk = lambda j: pl.BlockSpec((rows,d), lambda b, c: (b * cps + c, j))
    consts = list(wts) + [tri, bd]
    return pl.pallas_call(
        functools.partial(_scan_kernel, steps_per_seq=cps, cpb=cpb, n_valid=n_valid, use_inj=use_inj),
        grid=(n_steps, cps),
        in_specs=[colblk(0), colblk(1), colblk(2),
                  pl.BlockSpec((rows,LANE), lambda b, c: (b * cps + c, zz_blk)),
                  pl.BlockSpec((nsq, SUBLANE, d), lambda b, c: (b, 0, 0)),
                  pl.BlockSpec((nsq,) + s0.shape[1:], lambda b, c: (b, 0, 0, 0))]
                 + [full(a) for a in consts],
        out_specs=[pl.BlockSpec((rows,d), lambda b, c: (b * cps + c, 0)),
                   pl.BlockSpec((nsq,) + s0.shape[1:], lambda b, c: (b, 0, 0, 0))],
        out_shape=[jax.ShapeDtypeStruct((r, d), F32), jax.ShapeDtypeStruct(s0.shape, F32)],
        scratch_shapes=[pltpu.VMEM((nsq * npair, LANE, LANE), F32), pltpu.VMEM((SUBLANE, d), F32)],
        compiler_params=_cp(("arbitrary", "arbitrary")),
        name="rwkv_scan",
    )(z, z, z, z, inj, s0, *consts)


def _merge_kernel(oa_ref, ob_ref, ga_ref, gb_ref, x_ref, g1_ref, wo_ref, *rest, sample):
    if sample:
        wuv_ref, o_ref = rest
        kvr = wuv_ref.shape[1]
        oa = jnp.concatenate(
            [_dot(oa_ref[:, h * kvr:(h + 1) * kvr].astype(BF16), wuv_ref[h]) for h in range(A_HEADS)], axis=1)
    else:
        (o_ref,) = rest
        oa = oa_ref[...]
    merged = _sigmoid(ga_ref[...]) * oa + _sigmoid(gb_ref[...]) * ob_ref[...]
    o_ref[...] = x_ref[...] + g1_ref[...] * _dot(merged.astype(BF16), wo_ref[...])


def _merge(oa, ob, z, x, g1, wo, wuv, rows_per_batch, tm):
    r, d = x.shape
    sample = wuv is not None
    full = lambda a: pl.BlockSpec(a.shape, lambda i: (0,) * a.ndim)
    row = lambda n: pl.BlockSpec((tm, n), lambda i: (i, 0))
    in_specs = [row(oa.shape[1]), row(d), pl.BlockSpec((tm, d), lambda i: (i, 3)),
                pl.BlockSpec((tm, d), lambda i: (i, 4)), row(d), _mod_spec(g1, tm, rows_per_batch), full(wo)]
    args = [oa, ob, z, z, x, g1, wo]
    if sample:
        in_specs.append(full(wuv))
        args.append(wuv)
    return pl.pallas_call(
        functools.partial(_merge_kernel, sample=sample),
        grid=(r // tm,),
        in_specs=in_specs, out_specs=row(d),
        out_shape=jax.ShapeDtypeStruct((r, d), F32),
        compiler_params=_cp(("arbitrary",)),
        name="merge_out",
    )(*args)


def _ffn_kernel(x_ref, sc_ref, sh_ref, g2_ref, gn_ref, wu_ref, wv_ref, cw_ref, cb_ref, wd_ref, *rest,
                seq_len, sample, final):
    rest = list(rest)
    if sample:
        inj1_ref, inj2_ref = rest[:2]
        rest = rest[2:]
    if final:
        gf_ref = rest.pop(0)
    x2_ref, u_ref = rest[:2]
    rest = rest[2:]
    if final:
        y_ref = rest.pop(0)
    h_ref, acc_ref, carry_ref = rest
    i, j = pl.program_id(0), pl.program_id(1)
    tm = x_ref.shape[0]

    @pl.when(j == 0)
    def _():
        y = _rms(x_ref[...], gn_ref[...])
        h_ref[...] = (y * (1.0 + sc_ref[...]) + sh_ref[...]).astype(BF16)
        acc_ref[...] = jnp.zeros_like(acc_ref)

    h = h_ref[...]
    row = lax.broadcasted_iota(jnp.int32, (tm, 1), 0)
    if not sample:
        @pl.when(((i * tm) & (seq_len - 1)) == 0)
        def _():
            carry_ref[j] = jnp.zeros(carry_ref.shape[1:], F32)

    tf = wu_ref.shape[1]
    cols = [(c0, min(c0 + FFN_SUB, tf)) for c0 in range(0, tf, FFN_SUB)]
    us = [_dot(h, wu_ref[:, c0:c1]) for c0, c1 in cols]
    vals = [_dot(h, wv_ref[:, c0:c1]) for c0, c1 in cols]
    acts = []
    for (c0, c1), u, val in zip(cols, us, vals):
        r1, r2 = pltpu.roll(u, 1, axis=0), pltpu.roll(u, 2, axis=0)
        if sample:
            t = row & (seq_len - 1)
            u1 = jnp.where(t == 0, inj1_ref[:, c0:c1], r1)
            u2 = jnp.where(t < 2, inj2_ref[:, c0:c1], r2)
            u_ref[:, c0:c1] = u
        else:
            prev = carry_ref[j, :, c0:c1]
            u1 = jnp.where(row == 0, prev[7:8, :], r1)
            u2 = jnp.where(row == 0, prev[6:7, :], jnp.where(row == 1, prev[7:8, :], r2))
            carry_ref[j, :, c0:c1] = u[tm - SUBLANE:tm, :]
            u_ref[:, c0:c1] = u[tm - SUBLANE:tm, :]
        conv = cb_ref[:, c0:c1] + cw_ref[0:1, c0:c1] * u2 + cw_ref[1:2, c0:c1] * u1 + cw_ref[2:3, c0:c1] * u
        acts.append((conv * _sigmoid(conv) * val).astype(BF16))
    f = _dot(acts[0], wd_ref[cols[0][0]:cols[0][1], :])
    for (c0, c1), act in zip(cols[1:], acts[1:]):
        f = f + _dot(act, wd_ref[c0:c1, :])
    acc_ref[...] += f

    @pl.when(j == pl.num_programs(1) - 1)
    def _():
        x2 = x_ref[...] + g2_ref[...] * acc_ref[...]
        x2_ref[...] = x2
        if final:
            y_ref[...] = _rms(x2, gf_ref[...])


def _ffn(x, sc, sh, g2, gn, wu, wv, cw, cb, wd, inj, gf, *, rows_per_batch, seq_len, tm, tf):
    r, d = x.shape
    f = wu.shape[1]
    sample, final = inj is not None, gf is not None
    nf = f // tf
    cst = lambda a: pl.BlockSpec(a.shape, lambda i, j: (0,) * a.ndim)
    in_specs = [pl.BlockSpec((tm, d), lambda i, j: (i, 0)),
                _mod_spec(sc, tm, rows_per_batch), _mod_spec(sh, tm, rows_per_batch),
                _mod_spec(g2, tm, rows_per_batch), cst(gn),
                pl.BlockSpec((d, tf), lambda i, j: (0, j)), pl.BlockSpec((d, tf), lambda i, j: (0, j)),
                pl.BlockSpec((3, tf), lambda i, j: (0, j)), pl.BlockSpec((1, tf), lambda i, j: (0, j)),
                pl.BlockSpec((tf, d), lambda i, j: (j, 0))]
    args = [x, sc, sh, g2, gn, wu, wv, cw, cb, wd]
    if sample:
        in_specs += [pl.BlockSpec((tm, tf), lambda i, j: (i, j))] * 2
        args += list(inj)
    if final:
        in_specs.append(cst(gf))
        args.append(gf)
    out_specs = [pl.BlockSpec((tm, d), lambda i, j: (i, 0))]
    out_shape = [jax.ShapeDtypeStruct((r, d), F32)]
    if sample:
        out_specs.append(pl.BlockSpec((tm, tf), lambda i, j: (i, j)))
        out_shape.append(jax.ShapeDtypeStruct((r, f), F32))
    else:
        out_specs.append(pl.BlockSpec((None, SUBLANE, tf), lambda i, j: (i, 0, j)))
        out_shape.append(jax.ShapeDtypeStruct((r // tm, SUBLANE, f), F32))
    if final:
        out_specs.append(pl.BlockSpec((tm, d), lambda i, j: (i, 0)))
        out_shape.append(jax.ShapeDtypeStruct((r, d), F32))
    return pl.pallas_call(
        functools.partial(_ffn_kernel, seq_len=seq_len, sample=sample, final=final),
        grid=(r // tm, nf),
        in_specs=in_specs, out_specs=out_specs, out_shape=out_shape,
        scratch_shapes=[pltpu.VMEM((tm, d), BF16), pltpu.VMEM((tm, d), F32),
                        pltpu.VMEM((nf, SUBLANE, tf), F32)],
        compiler_params=_cp(("arbitrary", "arbitrary")),
        name="conv_ffn",
    )(*args)


def _pad_cols(a, n):
    return jnp.pad(a, ((0, 0), (0, n - a.shape[1])))


def _rot(a):
    half = a.shape[-1] // 2
    return jnp.concatenate([-a[..., half:], a[..., :half]], axis=-1)


def _layer_weights(l, w_in, mu_shift, w_uq, w_uk, w_uv, w2, a2, d, q_rank, kv_rank):
    dd = B_HEAD
    o_zb = q_rank + kv_rank + QK_ROPE
    o_r, o_zw, o_k, o_v, o_za = 0, d, d + dd, 2 * d + dd, 3 * d + dd
    n_zb = 3 * d + 2 * dd
    wi = w_in[l]
    zb = wi[:, o_zb:o_zb + n_zb]
    kpe_w = wi[:, q_rank + kv_rank:o_zb]
    misc = jnp.concatenate([wi[:, :q_rank + kv_rank], zb[:, o_zw:o_zw + dd], zb[:, o_za:o_za + dd],
                            _pad_cols(kpe_w, LANE), _pad_cols(_rot(kpe_w), LANE)], axis=1)
    w_z = jnp.concatenate([zb[:, o_r:o_r + d], zb[:, o_k:o_k + d], zb[:, o_v:o_v + d],
                           wi[:, o_zb + n_zb:o_zb + n_zb + 2 * d], misc], axis=1).astype(BF16)
    mu = mu_shift[l]
    mu_rkv = jnp.stack([mu[o_r:o_r + d], mu[o_k:o_k + d], mu[o_v:o_v + d]])
    mu_zz = jnp.concatenate([mu[o_zw:o_zw + dd], mu[o_za:o_za + dd]])[None]
    wq3 = w_uq[l].reshape(q_rank, A_HEADS, QK_NOPE + QK_ROPE)
    nope, rope = wq3[..., :QK_NOPE], wq3[..., QK_NOPE:]
    zpad = jnp.zeros((q_rank, A_HEADS, LANE - QK_NOPE - QK_ROPE), F32)
    wq = jnp.concatenate([rope, nope, zpad], axis=-1).reshape(q_rank, A_HEADS * LANE).astype(BF16)
    wqr = jnp.concatenate([_rot(rope), jnp.zeros_like(nope), zpad], axis=-1).reshape(q_rank, A_HEADS * LANE).astype(BF16)
    zk = jnp.zeros((kv_rank, A_HEADS, QK_ROPE), F32)
    wk = jnp.concatenate([zk, w_uk[l], zk], axis=-1).reshape(kv_rank, A_HEADS * LANE).astype(BF16)
    wv = w_uv[l].reshape(kv_rank, -1).T.astype(BF16)
    ukt = jnp.transpose(w_uk[l], (1, 2, 0))
    zr = jnp.zeros((A_HEADS, QK_ROPE, kv_rank), F32)
    wabs = jnp.concatenate([zr, ukt, zr], axis=1).astype(BF16)
    wuv_h = jnp.transpose(w_uv[l], (1, 0, 2)).astype(BF16)
    zrow = jnp.zeros((dd, d), F32)
    w2p = jnp.concatenate([w2[l], zrow], axis=0).astype(BF16)
    a2p = jnp.concatenate([zrow, a2[l]], axis=0).astype(BF16)
    return dict(w_z=w_z, mu_rkv=mu_rkv, mu_zz=mu_zz, wq=wq, wqr=wqr, wk=wk, wv=wv, wabs=wabs,
                wuv_h=wuv_h, w2p=w2p, a2p=a2p)


def _unperm_shift(zrow, d):
    dd = B_HEAD
    m = 5 * d
    zw, za = zrow[:, m + 640:m + 640 + dd], zrow[:, m + 640 + dd:m + 640 + 2 * dd]
    return jnp.concatenate([zrow[:, 0:d], zw, zrow[:, d:2 * d], zrow[:, 2 * d:3 * d], za], axis=1)


def _shift_inj(state_shift_l, d):
    dd = B_HEAD
    s = state_shift_l
    r, zw, k, v, za = s[:, 0:d], s[:, d:d + dd], s[:, d + dd:2 * d + dd], s[:, 2 * d + dd:3 * d + dd], s[:, 3 * d + dd:]
    zz = jnp.concatenate([zw, za, jnp.zeros((s.shape[0], d - 2 * dd), F32)], axis=1)
    rows = jnp.stack([r, k, v, zz], axis=1)
    return jnp.concatenate([rows, jnp.zeros((s.shape[0], SUBLANE - 4, d), F32)], axis=1)


def kernel(x_prompt, x_sample, c_prompt, c_sample, cache_ckv, cache_kpe, state_wkv, state_shift, state_conv, page_table, w_ada, b_ada, g_mix, g_ffn, w_in, g_q, w_uq, g_kv, w_uk, w_uv, mu_shift, w0, w2, a0, a2, k_k, k_a, r_k, lnx_w, lnx_b, w_out, w_up, conv_w, conv_b, w_down, g_final):
    bp, tp, d = x_prompt.shape
    bs, ts, _ = x_sample.shape
    depth = w_in.shape[0]
    q_rank, kv_rank = g_q.shape[1], g_kv.shape[1]
    d_ff = w_down.shape[1]
    n_pages, page = page_table.shape[1], cache_ckv.shape[2]
    past_len = n_pages * page
    heads_b = d // B_HEAD
    rp, rs = bp * tp, bs * S_PAD

    tm_p = min(512, tp)
    tm_s = min(256, rs)
    tq = min(1024, tp)
    kc, fl_sub, fl_unroll = min(128, tp), min(256, tq), 8
    chunk_p = min(64, tp)
    cpb_p = 4 if tp % (4 * chunk_p) == 0 else 1
    tf = d_ff // 2

    mod = _ada(jnp.concatenate([c_prompt, c_sample], axis=0), w_ada, b_ada)
    invf = ROPE_THETA ** (-jnp.arange(0, QK_ROPE, 2, dtype=F32) / QK_ROPE)
    invf = jnp.tile(invf, LANE // invf.shape[0])[None]

    cache_kpe_t = jnp.swapaxes(cache_kpe, 2, 3)
    xp = x_prompt.reshape(rp, d)
    xs = jnp.pad(x_sample, ((0, 0), (0, S_PAD - ts), (0, 0))).reshape(rs, d)
    new_p = [[] for _ in range(5)]
    new_s = [[] for _ in range(5)]
    yp = ys = None
    for l in range(depth):
        w = _layer_weights(l, w_in, mu_shift, w_uq, w_uk, w_uv, w2, a2, d, q_rank, kv_rank)
        mods_p = [m[:, None, :] for m in jnp.split(mod[l, :bp], 6, axis=-1)]
        mods_s = [jnp.repeat(m, S_PAD, axis=0) for m in jnp.split(mod[l, bp:], 6, axis=-1)]
        row1 = lambda a: a[l][None]
        scan_w = [w['mu_rkv'], w['mu_zz'], row1(w0), w['w2p'], row1(a0), w['a2p'], row1(k_k), row1(k_a),
                  r_k[l].reshape(1, d), row1(lnx_w), row1(lnx_b)]
        wu, wv_up = w_up[l][:, :d_ff].astype(BF16), w_up[l][:, d_ff:].astype(BF16)
        wd, wo = w_down[l].astype(BF16), w_out[l].astype(BF16)
        gf = g_final[None] if l == depth - 1 else None

        sh1, sc1, g1, sh2, sc2, g2 = mods_p
        z = _normproj(xp, sc1, sh1, row1(g_mix), w['w_z'], tp, min(1024, tp), 1536)
        ckv, kpe, q, k, vt = _prep(z, row1(g_q), w['wq'], w['wqr'], row1(g_kv), invf, [w['wk'], w['wv']],
                                   seq_len=tp, pos_base=0, sample=False, tm=tm_p, kv_chunk=kc)
        oa = _flash(q.reshape(bp, tp, d), k.reshape(bp, tp, d), vt, tq, kc, fl_sub, fl_unroll).reshape(rp, d)
        ob, sp = _scan(z, None, jnp.zeros((bp, heads_b, B_HEAD, B_HEAD), F32), scan_w,
                       n_seq=bp, seq_len=tp, chunk=chunk_p, cpb=cpb_p, n_valid=chunk_p)
        x1 = _merge(oa, ob, z, xp, g1, wo, None, tp, tm_p)
        outs = _ffn(x1, sc2, sh2, g2, row1(g_ffn), wu, wv_up, conv_w[l], row1(conv_b), wd, None, gf,
                    rows_per_batch=tp, seq_len=tp, tm=tm_p, tf=tf)
        xp, utail = outs[0], outs[1]
        if gf is not None:
            yp = outs[2]
        zlast = z.reshape(bp, tp, -1)[:, tp - 1]
        new_p[0].append(ckv.reshape(bp, tp, kv_rank))
        new_p[1].append(kpe.reshape(bp, tp, QK_ROPE))
        new_p[2].append(sp)
        new_p[3].append(_unperm_shift(zlast, d))
        new_p[4].append(utail.reshape(bp, tp // tm_p, SUBLANE, d_ff)[:, -1, SUBLANE - 2:])

        sh1, sc1, g1, sh2, sc2, g2 = mods_s
        z = _normproj(xs, sc1, sh1, row1(g_mix), w['w_z'], rs, tm_s, 1536)
        ckv, kpe, qlat, qpe = _prep(z, row1(g_q), w['wq'], w['wqr'], row1(g_kv), invf, [w['wabs']],
                                    seq_len=S_PAD, pos_base=past_len, sample=True, tm=tm_s)
        qlat = qlat.reshape(bs, S_PAD, A_HEADS, kv_rank)[:, :ts].reshape(bs, ts * A_HEADS, kv_rank)
        qpe = qpe.reshape(bs, S_PAD, A_HEADS, LANE)[:, :ts, :, :QK_ROPE].reshape(bs, ts * A_HEADS, QK_ROPE)
        ckv3, kpe3 = ckv.reshape(bs, S_PAD, kv_rank), kpe.reshape(bs, S_PAD, QK_ROPE)
        olat = _sattn(page_table, qlat, qpe, ckv3, kpe3, cache_ckv, cache_kpe_t, l, ts)
        olat = jnp.pad(olat.reshape(bs, ts, A_HEADS * kv_rank), ((0, 0), (0, S_PAD - ts), (0, 0)))
        ob, sp = _scan(z, _shift_inj(state_shift[l], d), state_wkv[l], scan_w,
                       n_seq=bs, seq_len=S_PAD, chunk=S_PAD, cpb=8 if bs % 8 == 0 else 1, n_valid=ts)
        x1 = _merge(olat.reshape(rs, -1), ob, z, xs, g1, wo, w['wuv_h'], rs, tm_s)
        cp = state_conv[l]
        zc = jnp.zeros((bs, S_PAD, d_ff), F32)
        inj1 = zc.at[:, 0].set(cp[:, 1]).reshape(rs, d_ff)
        inj2 = zc.at[:, 0].set(cp[:, 0]).at[:, 1].set(cp[:, 1]).reshape(rs, d_ff)
        outs = _ffn(x1, sc2, sh2, g2, row1(g_ffn), wu, wv_up, conv_w[l], row1(conv_b), wd, (inj1, inj2), gf,
                    rows_per_batch=rs, seq_len=S_PAD, tm=tm_s, tf=tf)
        xs, u_full = outs[0], outs[1]
        if gf is not None:
            ys = outs[2]
        new_s[0].append(ckv3[:, :ts])
        new_s[1].append(kpe3[:, :ts])
        new_s[2].append(sp)
        new_s[3].append(_unperm_shift(z.reshape(bs, S_PAD, -1)[:, ts - 1], d))
        new_s[4].append(u_full.reshape(bs, S_PAD, d_ff)[:, ts - 2:ts])

    stack = lambda u: jnp.stack(u, axis=0)
    return (yp.reshape(bp, tp, d), ys.reshape(bs, S_PAD, d)[:, :ts],
            stack(new_p[0]), stack(new_p[1]), stack(new_p[2]), stack(new_p[3]), stack(new_p[4]),
            stack(new_s[0]), stack(new_s[1]), stack(new_s[2]), stack(new_s[3]), stack(new_s[4]))
```

```python
import functools

import jax
import jax.numpy as jnp
from jax import lax
from jax.experimental import pallas as pl
from jax.experimental.pallas import tpu as pltpu

F32, BF16 = jnp.float32, jnp.bfloat16

EPS = 1e-6
GN_EPS = 64e-5
ROPE_THETA = 10000.0
A_HEADS = 8
QK_NOPE = 64
QK_ROPE = 32
B_HEAD = 64
ATTN_SCALE = (QK_NOPE + QK_ROPE) ** -0.5
LOG2E = 1.4426950408889634

LANE = 128
SUBLANE = 8
V7X_VMEM_LIMIT = 56 * 1024 * 1024
S_PAD = 8
FFN_SUB = 256


def _cp(sem):
    return pltpu.CompilerParams(dimension_semantics=sem, vmem_limit_bytes=V7X_VMEM_LIMIT)


def _dot(a, b):
    return jnp.dot(a, b, preferred_element_type=F32)


def _dot_nt(a, b):
    return lax.dot_general(a, b, (((1,), (1,)), ((), ())), preferred_element_type=F32)


def _dot_tn(a, b):
    return lax.dot_general(a, b, (((0,), (0,)), ((), ())), preferred_element_type=F32)


def _split2(x):
    hi = x.astype(BF16)
    return hi, (x - hi.astype(F32)).astype(BF16)


def _split3(x):
    hi = x.astype(BF16)
    r = x - hi.astype(F32)
    mid = r.astype(BF16)
    return hi, mid, (r - mid.astype(F32)).astype(BF16)


def _dot3(a, b):
    ah, al = _split2(a)
    bh, bl = _split2(b)
    return _dot(ah, bh) + (_dot(ah, bl) + _dot(al, bh))


def _sigmoid(x):
    return 1.0 / (1.0 + jnp.exp(-x))


def _rms(x, g):
    return x * lax.rsqrt(jnp.mean(x * x, axis=-1, keepdims=True) + EPS) * g


def _mod_spec(arr, tm, rows_per_batch):
    d = arr.shape[-1]
    if arr.ndim == 3:
        bpb = rows_per_batch // tm
        return pl.BlockSpec((None, 1, d), lambda i, *_: (i // bpb, 0, 0))
    return pl.BlockSpec((tm, d), lambda i, *_: (i, 0))


def _ada_kernel(c_ref, w_ref, b_ref, o_ref):
    c = c_ref[...]
    s = (c * _sigmoid(c)).astype(BF16)
    o_ref[...] = _dot(s, w_ref[...].astype(BF16)) + b_ref[...]


def _ada(c_all, w_ada, b_ada, tn=1536):
    depth, d, n = w_ada.shape
    bc = c_all.shape[0]
    return pl.pallas_call(
        _ada_kernel,
        grid=(depth, n // tn),
        in_specs=[pl.BlockSpec((bc, d), lambda l, j: (0, 0)),
                  pl.BlockSpec((None, d, tn), lambda l, j: (l, 0, j)),
                  pl.BlockSpec((None, 1, tn), lambda l, j: (l, 0, j))],
        out_specs=pl.BlockSpec((None, bc, tn), lambda l, j: (l, 0, j)),
        out_shape=jax.ShapeDtypeStruct((depth, bc, n), F32),
        compiler_params=_cp(("arbitrary", "arbitrary")),
        name="ada",
    )(c_all, w_ada, b_ada.reshape(depth, 1, n))


def _normproj_kernel(x_ref, sc_ref, sh_ref, g_ref, w_ref, o_ref, h_ref):
    @pl.when(pl.program_id(1) == 0)
    def _():
        y = _rms(x_ref[...], g_ref[...])
        h_ref[...] = (y * (1.0 + sc_ref[...]) + sh_ref[...]).astype(BF16)

    o_ref[...] = _dot(h_ref[...], w_ref[...]).astype(o_ref.dtype)


def _normproj(x, sc, sh, g, w, rows_per_batch, tm, tn, out_dtype):
    r, d = x.shape
    n = w.shape[1]
    return pl.pallas_call(
        _normproj_kernel,
        grid=(r // tm, n // tn),
        in_specs=[pl.BlockSpec((tm, d), lambda i, j: (i, 0)),
                  _mod_spec(sc, tm, rows_per_batch), _mod_spec(sh, tm, rows_per_batch),
                  pl.BlockSpec((1, d), lambda i, j: (0, 0)),
                  pl.BlockSpec((d, tn), lambda i, j: (0, j))],
        out_specs=pl.BlockSpec((tm, tn), lambda i, j: (i, j)),
        out_shape=jax.ShapeDtypeStruct((r, n), out_dtype),
        scratch_shapes=[pltpu.VMEM((tm, d), BF16)],
        compiler_params=_cp(("arbitrary", "arbitrary")),
        name="normproj",
    )(x, sc, sh, g, w)


def _prep_kernel(misc_ref, gq_ref, wq_ref, wqr_ref, gkv_ref, invf_ref, *rest,
                 seq_len, pos_base, q_rank, kv_rank, sample):
    tm = misc_ref.shape[0]
    i = pl.program_id(0)
    row = lax.broadcasted_iota(jnp.int32, (tm, LANE), 0) + i * tm
    pos = (pos_base + (row & (seq_len - 1))).astype(F32)
    ang = pos * invf_ref[...]
    cosv, sinv = jnp.cos(ang), jnp.sin(ang)
    lane = lax.broadcasted_iota(jnp.int32, (tm, LANE), 1)
    is_pe = lane < QK_ROPE
    ck = jnp.where(is_pe, cosv, 0.0)
    sk = jnp.where(is_pe, sinv, 0.0)
    qscale = ATTN_SCALE if sample else ATTN_SCALE * LOG2E
    cq = jnp.where(is_pe, cosv, jnp.where(lane < QK_ROPE + QK_NOPE, 1.0, 0.0)) * qscale
    sq = sk * qscale

    misc = lambda c0, c1: misc_ref[:, c0:c1].astype(F32)
    qn = _rms(misc(0, q_rank), gq_ref[...]).astype(BF16)
    q = _dot(qn, wq_ref[...])
    qr = _dot(qn, wqr_ref[...])
    ckv = _rms(misc(q_rank, q_rank + kv_rank), gkv_ref[...])
    ckv_b = ckv.astype(BF16)
    o_kpe = q_rank + kv_rank + LANE
    kpe = misc(o_kpe, o_kpe + LANE) * ck + misc(o_kpe + LANE, o_kpe + 2 * LANE) * sk

    if sample:
        wabs_ref, ckv_ref, kpe_ref, qlat_ref, qpe_ref = rest
        for h in range(A_HEADS):
            sl = slice(h * LANE, (h + 1) * LANE)
            qh = (q[:, sl] * cq + qr[:, sl] * sq).astype(BF16)
            qpe_ref[:, sl] = qh
            qlat_ref[:, h * kv_rank:(h + 1) * kv_rank] = _dot(qh, wabs_ref[h]).astype(BF16)
    else:
        wk_ref, wvt_ref, ckv_ref, kpe_ref, q_ref, k_ref, vt_ref = rest
        kn = _dot(ckv_b, wk_ref[...])
        for h in range(A_HEADS):
            sl = slice(h * LANE, (h + 1) * LANE)
            q_ref[:, sl] = (q[:, sl] * cq + qr[:, sl] * sq).astype(BF16)
            k_ref[:, sl] = (kn[:, sl] + kpe).astype(BF16)
        vt = _dot_nt(wvt_ref[...], ckv_b)
        kc = vt_ref.shape[2]
        for c in range(tm // kc):
            vt_ref[c] = vt[:, c * kc:(c + 1) * kc].astype(BF16)
    ckv_ref[...] = ckv
    kpe_ref[...] = kpe[:, 0:QK_ROPE]


def _prep(z, gq, wq, wqr, gkv, invf, extra, *, seq_len, pos_base, sample, tm, kv_chunk=None):
    r = z.shape[0]
    q_rank, kv_rank = gq.shape[1], gkv.shape[1]
    d = wq.shape[1]
    misc_blk = z.shape[1] // d - 1
    full = lambda a: pl.BlockSpec(a.shape, lambda i: (0,) * a.ndim)
    row = lambda n: pl.BlockSpec((tm, n), lambda i: (i, 0))
    in_specs = [pl.BlockSpec((tm, d), lambda i: (i, misc_blk)), full(gq), full(wq), full(wqr),
                full(gkv), full(invf)] + [full(a) for a in extra]
    out_specs = [row(kv_rank), row(QK_ROPE)]
    out_shape = [jax.ShapeDtypeStruct((r, kv_rank), F32), jax.ShapeDtypeStruct((r, QK_ROPE), F32)]
    if sample:
        out_specs += [row(A_HEADS * kv_rank), row(d)]
        out_shape += [jax.ShapeDtypeStruct((r, A_HEADS * kv_rank), BF16), jax.ShapeDtypeStruct((r, d), BF16)]
    else:
        out_specs += [row(d), row(d), pl.BlockSpec((tm // kv_chunk, d, kv_chunk), lambda i: (i, 0, 0))]
        out_shape += [jax.ShapeDtypeStruct((r, d), BF16)] * 2 + [jax.ShapeDtypeStruct((r // kv_chunk, d, kv_chunk), BF16)]
    return pl.pallas_call(
        functools.partial(_prep_kernel, seq_len=seq_len, pos_base=pos_base, q_rank=q_rank,
                          kv_rank=kv_rank, sample=sample),
        grid=(r // tm,),
        in_specs=in_specs, out_specs=out_specs, out_shape=out_shape,
        compiler_params=_cp(("arbitrary",)),
        name="mla_prep_sample" if sample else "mla_prep_prompt",
    )(z, gq, wq, wqr, gkv, invf, *extra)


def _flash_kernel(q_ref, k_ref, vt_ref, o_ref, *, tq, kc, sub, unroll, pv_depth):
    qi = pl.program_id(2)
    nsub = tq // sub
    qs = [q_ref[s * sub:(s + 1) * sub, :] for s in range(nsub)]

    def group(j0, tiles, carries):
        carries = list(carries)
        chunks = sorted({c for c, _, _ in tiles})
        ks = {c: k_ref[pl.ds(pl.multiple_of((j0 + c) * kc, kc), kc), :] for c in chunks}
        vts = {c: vt_ref[j0 + c] for c in chunks}
        sts = [_dot_nt(ks[c], qs[s]) for c, s, _ in tiles]
        sts = [st if mask is None else jnp.where(mask, st, -jnp.inf) for st, (_, _, mask) in zip(sts, tiles)]
        mloc = [jnp.max(st, axis=0, keepdims=True) for st in sts]
        units = []
        for s in range(nsub):
            idx = [i for i, (_, s_, _) in enumerate(tiles) if s_ == s]
            units += [(s, idx[i:i + pv_depth]) for i in range(0, len(idx), pv_depth)]
        ps, alphas = [], []
        for s, idx in units:
            m, l, acc = carries[s]
            m_new = m
            for i in idx:
                m_new = jnp.maximum(m_new, mloc[i])
            alpha = jnp.exp2(m - m_new)
            p = [jnp.exp2(sts[i] - m_new) for i in idx]
            l = alpha * l
            for x in p:
                l = l + jnp.sum(x, axis=0, keepdims=True)
            carries[s] = (m_new, l, acc)
            ps.append(jnp.concatenate([x.astype(BF16) for x in p], axis=0))
            alphas.append(alpha)
        pvs = [_dot(jnp.concatenate([vts[tiles[i][0]] for i in idx], axis=1), p)
               for (_, idx), p in zip(units, ps)]
        for (s, _), alpha, pv in zip(units, alphas, pvs):
            m, l, acc = carries[s]
            carries[s] = (m, l, alpha * acc + pv)
        return tuple(carries)

    def body(j, carries):
        return group(j * unroll, [(c, s, None) for c in range(unroll) for s in range(nsub)], carries)

    init = tuple((jnp.full((1, sub), -jnp.inf, F32), jnp.zeros((1, sub), F32),
                  jnp.zeros((vt_ref.shape[1], sub), F32)) for _ in range(nsub))
    per_q = tq // kc
    carries = lax.fori_loop(0, qi * (per_q // unroll), body, init)
    key = lax.broadcasted_iota(jnp.int32, (kc, sub), 0)
    qry = lax.broadcasted_iota(jnp.int32, (kc, sub), 1)
    for g0 in range(0, per_q, unroll):
        tiles = []
        for jj in range(g0, g0 + unroll):
            for s in range(nsub):
                if jj * kc > s * sub + sub - 1:
                    continue
                needs_mask = jj * kc + kc - 1 > s * sub
                tiles.append((jj, s, (key + jj * kc <= qry + s * sub) if needs_mask else None))
        carries = group(qi * per_q, tiles, carries)
    for s in range(nsub):
        _, l, acc = carries[s]
        o_ref[s * sub:(s + 1) * sub, :] = jnp.transpose(acc / l).astype(o_ref.dtype)


def _flash(q, k, vt, tq, kc, sub, unroll):
    b, t, d = q.shape
    return pl.pallas_call(
        functools.partial(_flash_kernel, tq=tq, kc=kc, sub=sub, unroll=unroll, pv_depth=1),
        grid=(b, A_HEADS, t // tq),
        in_specs=[pl.BlockSpec((None, tq, LANE), lambda b_, h, i: (b_, i, h)),
                  pl.BlockSpec((None, t, LANE), lambda b_, h, i: (b_, 0, h)),
                  pl.BlockSpec((t // kc, LANE, kc), lambda b_, h, i: (b_, h, 0))],
        out_specs=pl.BlockSpec((None, tq, LANE), lambda b_, h, i: (b_, i, h)),
        out_shape=jax.ShapeDtypeStruct((b, t, d), BF16),
        compiler_params=_cp(("arbitrary", "arbitrary", "arbitrary")),
        name="flash_prompt",
    )(q, k, vt)


def _sattn_kernel(pt_ref, qlat_ref, qpe_ref, knew_ref, pnew_ref, ckv_hbm, kpe_hbm, o_ref,
                  kbuf, pbuf, sem, *, layer, n_pages, page, n_valid, ck):
    b = pl.program_id(0)
    nb = pl.num_programs(0)
    slot = b % 2
    spb = qlat_ref.shape[0]

    def copies(step, s):
        out = []
        for i in range(spb):
            for p in range(n_pages):
                pg = pt_ref[(step * spb + i) * n_pages + p]
                out.append(pltpu.make_async_copy(ckv_hbm.at[layer, pg], kbuf.at[s, i, pl.ds(p * page, page)],
                                                 sem.at[0, s]))
                out.append(pltpu.make_async_copy(kpe_hbm.at[layer, pg], pbuf.at[s, i, p], sem.at[1, s]))
        return out

    @pl.when(b == 0)
    def _():
        for c in copies(0, 0):
            c.start()

    @pl.when(b + 1 < nb)
    def _():
        for c in copies(b + 1, 1 - slot):
            c.start()

    for c in copies(b, slot):
        c.wait()

    ppt = ck // page
    seqs, tiles = range(spb), range(n_pages // ppt)
    qlat = [qlat_ref[i] for i in seqs]
    qpe = [qpe_ref[i] for i in seqs]
    kcs = [[kbuf[slot, i, t * ck:(t + 1) * ck, :].astype(BF16) for t in tiles] for i in seqs]
    s_pe = [[jnp.concatenate([_dot(qpe[i], pbuf[slot, i, p].astype(BF16))
                              for p in range(t * ppt, (t + 1) * ppt)], axis=1) for t in tiles] for i in seqs]
    ss = [[_dot_nt(qlat[i], kcs[i][t]) + s_pe[i][t] for t in tiles] for i in seqs]
    kn = [knew_ref[i].astype(BF16) for i in seqs]
    pn = [pnew_ref[i].astype(BF16) for i in seqs]
    s_new = [_dot_nt(qlat[i], kn[i]) + _dot_nt(qpe[i], pn[i]) for i in seqs]
    tok = jnp.right_shift(lax.broadcasted_iota(jnp.int32, s_new[0].shape, 0), A_HEADS.bit_length() - 1)
    key = lax.broadcasted_iota(jnp.int32, s_new[0].shape, 1)
    s_new = [jnp.where((key <= tok) & (key < n_valid), x, -jnp.inf) for x in s_new]
    for i in seqs:
        m = jnp.max(s_new[i], axis=-1, keepdims=True)
        for s in ss[i]:
            m = jnp.maximum(m, jnp.max(s, axis=-1, keepdims=True))
        p_new = jnp.exp(s_new[i] - m)
        ps = [jnp.exp(s - m) for s in ss[i]]
        l = jnp.sum(p_new, axis=-1, keepdims=True)
        for p in ps:
            l = l + jnp.sum(p, axis=-1, keepdims=True)
        acc = _dot(p_new.astype(BF16), kn[i])
        for t in tiles:
            acc = acc + _dot(ps[t].astype(BF16), kcs[i][t])
        o_ref[i] = acc / l


def _sattn(page_table, qlat, qpe, knew, pnew, cache_ckv, cache_kpe_t, layer, n_valid):
    bs, nq, kvr = qlat.shape
    n_pages = page_table.shape[1]
    page = cache_ckv.shape[2]
    past = n_pages * page
    ck = min(512, past)
    spb = 2 if bs % 2 == 0 else 1
    blk = lambda a: pl.BlockSpec((spb,) + a.shape[1:], lambda b, pt: (b,) + (0,) * (a.ndim - 1))
    return pl.pallas_call(
        functools.partial(_sattn_kernel, layer=layer, n_pages=n_pages, page=page, n_valid=n_valid, ck=ck),
        grid_spec=pltpu.PrefetchScalarGridSpec(
            num_scalar_prefetch=1, grid=(bs // spb,),
            in_specs=[blk(qlat), blk(qpe), blk(knew), blk(pnew),
                      pl.BlockSpec(memory_space=pl.ANY), pl.BlockSpec(memory_space=pl.ANY)],
            out_specs=pl.BlockSpec((spb, nq, kvr), lambda b, pt: (b, 0, 0)),
            scratch_shapes=[pltpu.VMEM((2, spb, past, kvr), F32),
                            pltpu.VMEM((2, spb, n_pages) + cache_kpe_t.shape[2:], F32),
                            pltpu.SemaphoreType.DMA((2, 2))]),
        out_shape=jax.ShapeDtypeStruct((bs, nq, kvr), F32),
        compiler_params=_cp(("arbitrary",)),
        name="paged_attn_sample",
    )(page_table.reshape(-1), qlat, qpe, knew, pnew, cache_ckv, cache_kpe_t)


def _seg_sum(x, bd):
    rows, w = x.shape[0], bd.shape[0]
    n = x.shape[1] // w
    hi = x.astype(BF16).astype(F32)
    lo = x - hi
    lhs = jnp.concatenate([part[:, c * w:(c + 1) * w] for part in (hi, lo) for c in range(n)], axis=0)
    out = _dot(lhs.astype(BF16), bd)
    return jnp.concatenate([out[c * rows:(c + 1) * rows] + out[(n + c) * rows:(n + c + 1) * rows]
                            for c in range(n)], axis=1)


def _scan_kernel(r_ref, k_ref, v_ref, zz_ref, inj_ref, s0_ref,
                 mu_ref, muz_ref, w0_ref, w2_ref, a0_ref, a2_ref, kk_ref, ka_ref, rk_ref,
                 lw_ref, lb_ref, tri_ref, bd_ref,
                 ob_ref, sout_ref,
                 state, carry, *, steps_per_seq, cpb, n_valid, use_inj):
    c = pl.program_id(1)
    rows, d = r_ref.shape
    L = rows // cpb
    nsq = s0_ref.shape[0]
    npair = d // LANE
    row = lax.broadcasted_iota(jnp.int32, (rows, 1), 0)
    first = c == 0

    @pl.when(first)
    def _():
        zero = jnp.zeros((B_HEAD, B_HEAD), F32)
        for sq in range(nsq):
            for p in range(npair):
                state[sq * npair + p] = jnp.concatenate(
                    [jnp.concatenate([s0_ref[sq, 2 * p], zero], axis=1),
                     jnp.concatenate([zero, s0_ref[sq, 2 * p + 1]], axis=1)], axis=0)
        carry[...] = jnp.zeros_like(carry)

    def shift(x, idx, mu):
        w = x.shape[1]
        if use_inj:
            prev_rows = [inj_ref[sq, idx:idx + 1, 0:w] for sq in range(nsq)]
            prev_row = prev_rows[0] if nsq == 1 else jnp.concatenate(
                [jnp.broadcast_to(x_, (L, w)) for x_ in prev_rows], axis=0)
            prev = jnp.where((row & (L - 1)) == 0, prev_row, pltpu.roll(x, 1, axis=0))
        else:
            prev = jnp.where(row == 0, carry[idx:idx + 1, 0:w], pltpu.roll(x, 1, axis=0))
        if not use_inj:
            carry[idx:idx + 1, 0:x.shape[1]] = x[rows - 1:rows, :]
        return x + (prev - x) * mu

    r = shift(r_ref[...].astype(F32), 0, mu_ref[0:1, :])
    k = shift(k_ref[...].astype(F32), 1, mu_ref[1:2, :])
    v = shift(v_ref[...].astype(F32), 2, mu_ref[2:3, :])
    zz = shift(zz_ref[...].astype(F32), 3, muz_ref[...])

    wl = w0_ref[...] + _dot(jnp.tanh(zz).astype(BF16), w2_ref[...])
    sp = jnp.maximum(-wl, 0.0) + jnp.log(1.0 + jnp.exp(-jnp.abs(wl)))
    logw = -jnp.exp(-sp - 0.5)
    a_c = _sigmoid(a0_ref[...] + _dot(zz.astype(BF16), a2_ref[...]))
    kkr = k * kk_ref[...]
    bd = bd_ref[...]
    kk = kkr * lax.rsqrt(jnp.maximum(_seg_sum(kkr * kkr, bd), 1e-24))
    k = k * (1.0 + (a_c - 1.0) * ka_ref[...])
    if n_valid < L:
        valid = (row & (L - 1)) < n_valid
        logw = jnp.where(valid, logw, 0.0)
        kk = jnp.where(valid, kk, 0.0)
        k = jnp.where(valid, k, 0.0)
        v = jnp.where(valid, v, 0.0)
    bonus = _seg_sum(r * k * rk_ref[...], bd) * v

    l1, l2, l3 = _split3(logw)
    tri = tri_ref[...]
    cum = _dot(tri, l1) + (_dot(tri, l2) + _dot(tri, l3))
    cum_ends = [cum[(q + 1) * L - 1:(q + 1) * L, :] for q in range(cpb)]
    cum_end = cum_ends[0] if cpb == 1 else jnp.concatenate(
        [jnp.broadcast_to(x, (L, d)) for x in cum_ends], axis=0)
    e_fwd = jnp.exp(cum)
    e_inv = jnp.exp(-cum)
    e_end = jnp.exp(cum_end - cum)
    rt = r * e_fwd
    at = -kk * jnp.exp(cum - logw)
    b = kk * a_c
    bt, kt = b * e_inv, k * e_inv
    bh, kh = b * e_end, k * e_end
    g_ls = [jnp.exp(x) for x in cum_ends]

    lane = lax.broadcasted_iota(jnp.int32, (1, LANE), 1)
    m_a = lane < B_HEAD
    tr = lax.broadcasted_iota(jnp.int32, (2 * L, 2 * L), 0)
    tc = lax.broadcasted_iota(jnp.int32, (2 * L, 2 * L), 1)
    strict, incl, eye = tr > tc, tr >= tc, (tr == tc).astype(F32)

    def stack(x):
        return jnp.concatenate([jnp.where(m_a, x, 0.0), jnp.where(m_a, 0.0, x)], axis=0).astype(BF16)

    pairs = range(npair)
    sls = [slice(p * LANE, (p + 1) * LANE) for p in pairs]
    cat = lambda x, y_: jnp.concatenate([x, y_], axis=0)
    H = 2 * L

    def prepare(q):
        rs = slice(q * L, (q + 1) * L)
        ar_s = [cat(stack(at[rs, sl]), stack(rt[rs, sl])) for sl in sls]
        bk_s = [cat(stack(bt[rs, sl]), stack(kt[rs, sl])) for sl in sls]
        v_s = [stack(v[rs, sl]) for sl in sls]
        bkh_s = [cat(stack(bh[rs, sl]), stack(kh[rs, sl])) for sl in sls]
        if H % LANE == 0:
            g = [_dot_nt(ar_s[p], bk_s[p]) for p in pairs]
            g4 = [(x[0:H, 0:H], x[0:H, H:], x[H:, 0:H], x[H:, H:]) for x in g]
        else:
            g4 = [(_dot_nt(ar_s[p][0:H], bk_s[p][0:H]), _dot_nt(ar_s[p][0:H], bk_s[p][H:]),
                   _dot_nt(ar_s[p][H:], bk_s[p][0:H]), _dot_nt(ar_s[p][H:], bk_s[p][H:])) for p in pairs]
        a_ab = [jnp.where(strict, x[0], 0.0) for x in g4]
        akrk = [cat(jnp.where(strict, x[1], 0.0).astype(BF16), jnp.where(incl, x[3], 0.0).astype(BF16))
                for x in g4]
        a_rb = [jnp.where(incl, x[2], 0.0).astype(BF16) for x in g4]
        t_inv = [eye + x for x in a_ab]
        pw = [x.astype(BF16) for x in a_ab]
        pw = [_dot(x, x).astype(BF16) for x in pw]
        n = 2
        while n < L:
            last = 2 * n >= L
            if H % LANE == 0 and not last:
                pr = [_dot(pw[p], jnp.concatenate([pw[p], t_inv[p].astype(BF16)], axis=1)) for p in pairs]
                t_inv = [t_inv[p] + pr[p][:, H:] for p in pairs]
                pw = [x[:, 0:H].astype(BF16) for x in pr]
            else:
                t_inv = [t_inv[p] + _dot(pw[p], t_inv[p].astype(BF16)) for p in pairs]
                if not last:
                    pw = [_dot(x, x).astype(BF16) for x in pw]
            n *= 2
        return ar_s, v_s, bkh_s, akrk, a_rb, [x.astype(BF16) for x in t_inv]

    def advance(q, pre):
        ar_s, v_s, bkh_s, akrk, a_rb, t_inv = pre
        s_at = (q if nsq > 1 else 0) * npair
        st_f = [state[s_at + p] for p in pairs]
        ars = [_dot_nt(ar_s[p], st_f[p].astype(BF16)) for p in pairs]
        av = [_dot(akrk[p], v_s[p]) for p in pairs]
        u_b = [_dot(t_inv[p], (ars[p][0:H] + av[p][0:H]).astype(BF16)).astype(BF16) for p in pairs]
        y_s = [ars[p][H:] + av[p][H:] + _dot(a_rb[p], u_b[p]) for p in pairs]
        for p in pairs:
            state[s_at + p] = st_f[p] * g_ls[q][:, sls[p]] + _dot_tn(cat(u_b[p], v_s[p]), bkh_s[p])
        return jnp.concatenate([x[0:L] + x[L:] for x in y_s], axis=1)

    pres = [prepare(q) for q in range(cpb)]
    ys = [advance(q, pres[q]) for q in range(cpb)]
    y = ys[0] if cpb == 1 else jnp.concatenate(ys, axis=0)

    bdm = bd * (1.0 / B_HEAD)
    mean = _seg_sum(y, bdm)
    yc = y - mean
    var = _seg_sum(yc * yc, bdm)
    ob_ref[...] = (yc * lax.rsqrt(var + GN_EPS) * lw_ref[...] + lb_ref[...] + bonus).astype(ob_ref.dtype)

    @pl.when(c == steps_per_seq - 1)
    def _():
        for sq in range(nsq):
            for p in range(npair):
                sp = state[sq * npair + p]
                sout_ref[sq, 2 * p] = sp[0:B_HEAD, 0:B_HEAD]
                sout_ref[sq, 2 * p + 1] = sp[B_HEAD:, B_HEAD:]


def _scan(z, inj, s0, wts, *, n_seq, seq_len, chunk, cpb, n_valid):
    r = z.shape[0]
    d = wts[0].shape[1]
    npair = d // LANE
    rows = chunk * cpb
    use_inj = inj is not None
    if use_inj:
        assert seq_len == chunk and n_seq % cpb == 0
        n_steps, cps, nsq = n_seq // cpb, 1, cpb
    else:
        n_steps, cps, nsq = n_seq, seq_len // rows, 1
        inj = jnp.zeros((n_seq, SUBLANE, d), F32)
    zz_blk = (5 * d + 640) // LANE
    ia = jnp.arange(rows)
    tri = ((ia[:, None] >= ia[None, :]) & (ia[:, None] // chunk == ia[None, :] // chunk)).astype(BF16)
    bd = (jnp.arange(2 * LANE)[:, None] // B_HEAD == jnp.arange(2 * LANE)[None, :] // B_HEAD).astype(BF16)
    full = lambda a: pl.BlockSpec(a.shape, lambda b, c: (0,) * a.ndim)
    colblk = lambda j: pl.BlockSpec((rows,d), lambda b, c: (b * cps + c, j))
    consts = list(wts) + [tri, bd]
    return pl.pallas_call(
        functools.partial(_scan_kernel, steps_per_seq=cps, cpb=cpb, n_valid=n_valid, use_inj=use_inj),
        grid=(n_steps, cps),
        in_specs=[colblk(0), colblk(1), colblk(2),
                  pl.BlockSpec((rows,LANE), lambda b, c: (b * cps + c, zz_blk)),
                  pl.BlockSpec((nsq, SUBLANE, d), lambda b, c: (b, 0, 0)),
                  pl.BlockSpec((nsq,) + s0.shape[1:], lambda b, c: (b, 0, 0, 0))]
                 + [full(a) for a in consts],
        out_specs=[pl.BlockSpec((rows,d), lambda b, c: (b * cps + c, 0)),
                   pl.BlockSpec((nsq,) + s0.shape[1:], lambda b, c: (b, 0, 0, 0))],
        out_shape=[jax.ShapeDtypeStruct((r, d), BF16), jax.ShapeDtypeStruct(s0.shape, F32)],
        scratch_shapes=[pltpu.VMEM((nsq * npair, LANE, LANE), F32), pltpu.VMEM((SUBLANE, d), F32)],
        compiler_params=_cp(("arbitrary", "arbitrary")),
        name="rwkv_scan",
    )(z, z, z, z, inj, s0, *consts)


def _merge_kernel(oa_ref, ob_ref, ga_ref, gb_ref, x_ref, g1_ref, wo_ref, *rest, sample):
    if sample:
        wuv_ref, o_ref = rest
        kvr = wuv_ref.shape[1]
        oa = jnp.concatenate(
            [_dot(oa_ref[:, h * kvr:(h + 1) * kvr].astype(BF16), wuv_ref[h]) for h in range(A_HEADS)], axis=1)
    else:
        (o_ref,) = rest
        oa = oa_ref[...].astype(F32)
    merged = (_sigmoid(ga_ref[...].astype(F32)) * oa
              + _sigmoid(gb_ref[...].astype(F32)) * ob_ref[...].astype(F32))
    o_ref[...] = x_ref[...] + g1_ref[...] * _dot(merged.astype(BF16), wo_ref[...])


def _merge(oa, ob, z, x, g1, wo, wuv, rows_per_batch, tm):
    r, d = x.shape
    sample = wuv is not None
    full = lambda a: pl.BlockSpec(a.shape, lambda i: (0,) * a.ndim)
    row = lambda n: pl.BlockSpec((tm, n), lambda i: (i, 0))
    in_specs = [row(oa.shape[1]), row(d), pl.BlockSpec((tm, d), lambda i: (i, 3)),
                pl.BlockSpec((tm, d), lambda i: (i, 4)), row(d), _mod_spec(g1, tm, rows_per_batch), full(wo)]
    args = [oa, ob, z, z, x, g1, wo]
    if sample:
        in_specs.append(full(wuv))
        args.append(wuv)
    return pl.pallas_call(
        functools.partial(_merge_kernel, sample=sample),
        grid=(r // tm,),
        in_specs=in_specs, out_specs=row(d),
        out_shape=jax.ShapeDtypeStruct((r, d), F32),
        compiler_params=_cp(("arbitrary",)),
        name="merge_out",
    )(*args)


def _ffn_kernel(x_ref, sc_ref, sh_ref, g2_ref, gn_ref, wu_ref, wv_ref, cw_ref, cb_ref, wd_ref, *rest,
                seq_len, sample, final):
    rest = list(rest)
    if sample:
        cp_ref = rest.pop(0)
    if final:
        gf_ref = rest.pop(0)
    x2_ref, u_ref = rest[:2]
    rest = rest[2:]
    if final:
        y_ref = rest.pop(0)
    h_ref, acc_ref, carry_ref = rest
    i, j = pl.program_id(0), pl.program_id(1)
    tm = x_ref.shape[0]

    @pl.when(j == 0)
    def _():
        y = _rms(x_ref[...], gn_ref[...])
        h_ref[...] = (y * (1.0 + sc_ref[...]) + sh_ref[...]).astype(BF16)
        acc_ref[...] = jnp.zeros_like(acc_ref)

    h = h_ref[...]
    row = lax.broadcasted_iota(jnp.int32, (tm, 1), 0)
    if not sample:
        @pl.when(((i * tm) & (seq_len - 1)) == 0)
        def _():
            carry_ref[j] = jnp.zeros(carry_ref.shape[1:], F32)

    tf = wu_ref.shape[1]
    cols = [(c0, min(c0 + FFN_SUB, tf)) for c0 in range(0, tf, FFN_SUB)]
    us = [_dot(h, wu_ref[:, c0:c1]) for c0, c1 in cols]
    vals = [_dot(h, wv_ref[:, c0:c1]) for c0, c1 in cols]
    acts = []
    for (c0, c1), u, val in zip(cols, us, vals):
        r1, r2 = pltpu.roll(u, 1, axis=0), pltpu.roll(u, 2, axis=0)
        if sample:
            t = row & (seq_len - 1)
            cp = cp_ref[:, :, c0:c1]
            per_row = lambda x: jnp.broadcast_to(x, (tm // seq_len, seq_len, c1 - c0)).reshape(tm, c1 - c0)
            p0, p1 = per_row(cp[:, 0:1, :]), per_row(cp[:, 1:2, :])
            u1 = jnp.where(t == 0, p1, r1)
            u2 = jnp.where(t == 0, p0, jnp.where(t == 1, p1, r2))
            u_ref[:, c0:c1] = u
        else:
            prev = carry_ref[j, :, c0:c1]
            u1 = jnp.where(row == 0, prev[7:8, :], r1)
            u2 = jnp.where(row == 0, prev[6:7, :], jnp.where(row == 1, prev[7:8, :], r2))
            carry_ref[j, :, c0:c1] = u[tm - SUBLANE:tm, :]
            u_ref[:, c0:c1] = u[tm - SUBLANE:tm, :]
        conv = cb_ref[:, c0:c1] + cw_ref[0:1, c0:c1] * u2 + cw_ref[1:2, c0:c1] * u1 + cw_ref[2:3, c0:c1] * u
        acts.append((conv * _sigmoid(conv) * val).astype(BF16))
    f = _dot(acts[0], wd_ref[cols[0][0]:cols[0][1], :])
    for (c0, c1), act in zip(cols[1:], acts[1:]):
        f = f + _dot(act, wd_ref[c0:c1, :])
    acc_ref[...] += f

    @pl.when(j == pl.num_programs(1) - 1)
    def _():
        x2 = x_ref[...] + g2_ref[...] * acc_ref[...]
        x2_ref[...] = x2
        if final:
            y_ref[...] = _rms(x2, gf_ref[...])


def _ffn(x, sc, sh, g2, gn, wu, wv, cw, cb, wd, inj, gf, *, rows_per_batch, seq_len, tm, tf):
    r, d = x.shape
    f = wu.shape[1]
    sample, final = inj is not None, gf is not None
    nf = f // tf
    cst = lambda a: pl.BlockSpec(a.shape, lambda i, j: (0,) * a.ndim)
    in_specs = [pl.BlockSpec((tm, d), lambda i, j: (i, 0)),
                _mod_spec(sc, tm, rows_per_batch), _mod_spec(sh, tm, rows_per_batch),
                _mod_spec(g2, tm, rows_per_batch), cst(gn),
                pl.BlockSpec((d, tf), lambda i, j: (0, j)), pl.BlockSpec((d, tf), lambda i, j: (0, j)),
                pl.BlockSpec((3, tf), lambda i, j: (0, j)), pl.BlockSpec((1, tf), lambda i, j: (0, j)),
                pl.BlockSpec((tf, d), lambda i, j: (j, 0))]
    args = [x, sc, sh, g2, gn, wu, wv, cw, cb, wd]
    if sample:
        in_specs.append(pl.BlockSpec((tm // seq_len, inj.shape[1], tf), lambda i, j: (i, 0, j)))
        args.append(inj)
    if final:
        in_specs.append(cst(gf))
        args.append(gf)
    out_specs = [pl.BlockSpec((tm, d), lambda i, j: (i, 0))]
    out_shape = [jax.ShapeDtypeStruct((r, d), F32)]
    if sample:
        out_specs.append(pl.BlockSpec((tm, tf), lambda i, j: (i, j)))
        out_shape.append(jax.ShapeDtypeStruct((r, f), F32))
    else:
        out_specs.append(pl.BlockSpec((None, SUBLANE, tf), lambda i, j: (i, 0, j)))
        out_shape.append(jax.ShapeDtypeStruct((r // tm, SUBLANE, f), F32))
    if final:
        out_specs.append(pl.BlockSpec((tm, d), lambda i, j: (i, 0)))
        out_shape.append(jax.ShapeDtypeStruct((r, d), F32))
    return pl.pallas_call(
        functools.partial(_ffn_kernel, seq_len=seq_len, sample=sample, final=final),
        grid=(r // tm, nf),
        in_specs=in_specs, out_specs=out_specs, out_shape=out_shape,
        scratch_shapes=[pltpu.VMEM((tm, d), BF16), pltpu.VMEM((tm, d), F32),
                        pltpu.VMEM((nf, SUBLANE, tf), F32)],
        compiler_params=_cp(("arbitrary", "arbitrary")),
        name="conv_ffn",
    )(*args)


def _pad_cols(a, n):
    return jnp.pad(a, ((0, 0), (0, n - a.shape[1])))


def _rot(a):
    half = a.shape[-1] // 2
    return jnp.concatenate([-a[..., half:], a[..., :half]], axis=-1)


def _layer_weights(l, w_in, mu_shift, w_uq, w_uk, w_uv, w2, a2, d, q_rank, kv_rank):
    dd = B_HEAD
    o_zb = q_rank + kv_rank + QK_ROPE
    o_r, o_zw, o_k, o_v, o_za = 0, d, d + dd, 2 * d + dd, 3 * d + dd
    n_zb = 3 * d + 2 * dd
    wi = w_in[l]
    zb = wi[:, o_zb:o_zb + n_zb]
    kpe_w = wi[:, q_rank + kv_rank:o_zb]
    misc = jnp.concatenate([wi[:, :q_rank + kv_rank], zb[:, o_zw:o_zw + dd], zb[:, o_za:o_za + dd],
                            _pad_cols(kpe_w, LANE), _pad_cols(_rot(kpe_w), LANE)], axis=1)
    w_z = jnp.concatenate([zb[:, o_r:o_r + d], zb[:, o_k:o_k + d], zb[:, o_v:o_v + d],
                           wi[:, o_zb + n_zb:o_zb + n_zb + 2 * d], misc], axis=1).astype(BF16)
    mu = mu_shift[l]
    mu_rkv = jnp.stack([mu[o_r:o_r + d], mu[o_k:o_k + d], mu[o_v:o_v + d]])
    mu_zz = jnp.concatenate([mu[o_zw:o_zw + dd], mu[o_za:o_za + dd]])[None]
    wq3 = w_uq[l].reshape(q_rank, A_HEADS, QK_NOPE + QK_ROPE)
    nope, rope = wq3[..., :QK_NOPE], wq3[..., QK_NOPE:]
    zpad = jnp.zeros((q_rank, A_HEADS, LANE - QK_NOPE - QK_ROPE), F32)
    wq = jnp.concatenate([rope, nope, zpad], axis=-1).reshape(q_rank, A_HEADS * LANE).astype(BF16)
    wqr = jnp.concatenate([_rot(rope), jnp.zeros_like(nope), zpad], axis=-1).reshape(q_rank, A_HEADS * LANE).astype(BF16)
    zk = jnp.zeros((kv_rank, A_HEADS, QK_ROPE), F32)
    wk = jnp.concatenate([zk, w_uk[l], zk], axis=-1).reshape(kv_rank, A_HEADS * LANE).astype(BF16)
    wv = w_uv[l].reshape(kv_rank, -1).T.astype(BF16)
    ukt = jnp.transpose(w_uk[l], (1, 2, 0))
    zr = jnp.zeros((A_HEADS, QK_ROPE, kv_rank), F32)
    wabs = jnp.concatenate([zr, ukt, zr], axis=1).astype(BF16)
    wuv_h = jnp.transpose(w_uv[l], (1, 0, 2)).astype(BF16)
    zrow = jnp.zeros((dd, d), F32)
    w2p = jnp.concatenate([w2[l], zrow], axis=0).astype(BF16)
    a2p = jnp.concatenate([zrow, a2[l]], axis=0).astype(BF16)
    return dict(w_z=w_z, mu_rkv=mu_rkv, mu_zz=mu_zz, wq=wq, wqr=wqr, wk=wk, wv=wv, wabs=wabs,
                wuv_h=wuv_h, w2p=w2p, a2p=a2p)


def _unperm_shift(zrow, d):
    dd = B_HEAD
    m = 5 * d
    zw, za = zrow[:, m + 640:m + 640 + dd], zrow[:, m + 640 + dd:m + 640 + 2 * dd]
    return jnp.concatenate([zrow[:, 0:d], zw, zrow[:, d:2 * d], zrow[:, 2 * d:3 * d], za], axis=1)


def _shift_inj(state_shift_l, d):
    dd = B_HEAD
    s = state_shift_l
    r, zw, k, v, za = s[:, 0:d], s[:, d:d + dd], s[:, d + dd:2 * d + dd], s[:, 2 * d + dd:3 * d + dd], s[:, 3 * d + dd:]
    zz = jnp.concatenate([zw, za, jnp.zeros((s.shape[0], d - 2 * dd), F32)], axis=1)
    rows = jnp.stack([r, k, v, zz], axis=1)
    return jnp.concatenate([rows, jnp.zeros((s.shape[0], SUBLANE - 4, d), F32)], axis=1)


def kernel(x_prompt, x_sample, c_prompt, c_sample, cache_ckv, cache_kpe, state_wkv, state_shift, state_conv, page_table, w_ada, b_ada, g_mix, g_ffn, w_in, g_q, w_uq, g_kv, w_uk, w_uv, mu_shift, w0, w2, a0, a2, k_k, k_a, r_k, lnx_w, lnx_b, w_out, w_up, conv_w, conv_b, w_down, g_final):
    bp, tp, d = x_prompt.shape
    bs, ts, _ = x_sample.shape
    depth = w_in.shape[0]
    q_rank, kv_rank = g_q.shape[1], g_kv.shape[1]
    d_ff = w_down.shape[1]
    n_pages, page = page_table.shape[1], cache_ckv.shape[2]
    past_len = n_pages * page
    heads_b = d // B_HEAD
    rp, rs = bp * tp, bs * S_PAD

    tm_p = min(512, tp)
    tm_s = min(256, rs)
    tq = min(1024, tp)
    kc, fl_sub, fl_unroll = min(128, tp), min(256, tq), 8
    chunk_p = min(64, tp)
    cpb_p = 4 if tp % (4 * chunk_p) == 0 else 1
    tf = d_ff // 2

    mod = _ada(jnp.concatenate([c_prompt, c_sample], axis=0), w_ada, b_ada)
    invf = ROPE_THETA ** (-jnp.arange(0, QK_ROPE, 2, dtype=F32) / QK_ROPE)
    invf = jnp.tile(invf, LANE // invf.shape[0])[None]

    cache_kpe_t = jnp.swapaxes(cache_kpe, 2, 3)
    xp = x_prompt.reshape(rp, d)
    xs = jnp.pad(x_sample, ((0, 0), (0, S_PAD - ts), (0, 0))).reshape(rs, d)
    new_p = [[] for _ in range(5)]
    new_s = [[] for _ in range(5)]
    yp = ys = None
    for l in range(depth):
        w = _layer_weights(l, w_in, mu_shift, w_uq, w_uk, w_uv, w2, a2, d, q_rank, kv_rank)
        mods_p = [m[:, None, :] for m in jnp.split(mod[l, :bp], 6, axis=-1)]
        mods_s = [jnp.repeat(m, S_PAD, axis=0) for m in jnp.split(mod[l, bp:], 6, axis=-1)]
        row1 = lambda a: a[l][None]
        scan_w = [w['mu_rkv'], w['mu_zz'], row1(w0), w['w2p'], row1(a0), w['a2p'], row1(k_k), row1(k_a),
                  r_k[l].reshape(1, d), row1(lnx_w), row1(lnx_b)]
        wu, wv_up = w_up[l][:, :d_ff].astype(BF16), w_up[l][:, d_ff:].astype(BF16)
        wd, wo = w_down[l].astype(BF16), w_out[l].astype(BF16)
        gf = g_final[None] if l == depth - 1 else None

        sh1, sc1, g1, sh2, sc2, g2 = mods_p
        z = _normproj(xp, sc1, sh1, row1(g_mix), w['w_z'], tp, min(1024, tp), 1536, BF16)
        ckv, kpe, q, k, vt = _prep(z, row1(g_q), w['wq'], w['wqr'], row1(g_kv), invf, [w['wk'], w['wv']],
                                   seq_len=tp, pos_base=0, sample=False, tm=tm_p, kv_chunk=kc)
        oa = _flash(q.reshape(bp, tp, d), k.reshape(bp, tp, d), vt, tq, kc, fl_sub, fl_unroll).reshape(rp, d)
        ob, sp = _scan(z, None, jnp.zeros((bp, heads_b, B_HEAD, B_HEAD), F32), scan_w,
                       n_seq=bp, seq_len=tp, chunk=chunk_p, cpb=cpb_p, n_valid=chunk_p)
        x1 = _merge(oa, ob, z, xp, g1, wo, None, tp, tm_p)
        outs = _ffn(x1, sc2, sh2, g2, row1(g_ffn), wu, wv_up, conv_w[l], row1(conv_b), wd, None, gf,
                    rows_per_batch=tp, seq_len=tp, tm=tm_p, tf=tf)
        xp, utail = outs[0], outs[1]
        if gf is not None:
            yp = outs[2]
        zlast = z.reshape(bp, tp, -1)[:, tp - 1].astype(F32)
        new_p[0].append(ckv.reshape(bp, tp, kv_rank))
        new_p[1].append(kpe.reshape(bp, tp, QK_ROPE))
        new_p[2].append(sp)
        new_p[3].append(_unperm_shift(zlast, d))
        new_p[4].append(utail.reshape(bp, tp // tm_p, SUBLANE, d_ff)[:, -1, SUBLANE - 2:])

        sh1, sc1, g1, sh2, sc2, g2 = mods_s
        z = _normproj(xs, sc1, sh1, row1(g_mix), w['w_z'], rs, tm_s, 1536, F32)
        ckv, kpe, qlat, qpe = _prep(z, row1(g_q), w['wq'], w['wqr'], row1(g_kv), invf, [w['wabs']],
                                    seq_len=S_PAD, pos_base=past_len, sample=True, tm=tm_s)
        qlat = qlat.reshape(bs, S_PAD, A_HEADS, kv_rank)[:, :ts].reshape(bs, ts * A_HEADS, kv_rank)
        qpe = qpe.reshape(bs, S_PAD, A_HEADS, LANE)[:, :ts, :, :QK_ROPE].reshape(bs, ts * A_HEADS, QK_ROPE)
        ckv3, kpe3 = ckv.reshape(bs, S_PAD, kv_rank), kpe.reshape(bs, S_PAD, QK_ROPE)
        olat = _sattn(page_table, qlat, qpe, ckv3, kpe3, cache_ckv, cache_kpe_t, l, ts)
        olat = jnp.pad(olat.reshape(bs, ts, A_HEADS * kv_rank), ((0, 0), (0, S_PAD - ts), (0, 0)))
        ob, sp = _scan(z, _shift_inj(state_shift[l], d), state_wkv[l], scan_w,
                       n_seq=bs, seq_len=S_PAD, chunk=S_PAD, cpb=8 if bs % 8 == 0 else 1, n_valid=ts)
        x1 = _merge(olat.reshape(rs, -1), ob, z, xs, g1, wo, w['wuv_h'], rs, tm_s)
        outs = _ffn(x1, sc2, sh2, g2, row1(g_ffn), wu, wv_up, conv_w[l], row1(conv_b), wd, state_conv[l], gf,
                    rows_per_batch=rs, seq_len=S_PAD, tm=tm_s, tf=tf)
        xs, u_full = outs[0], outs[1]
        if gf is not None:
            ys = outs[2]
        new_s[0].append(ckv3[:, :ts])
        new_s[1].append(kpe3[:, :ts])
        new_s[2].append(sp)
        new_s[3].append(_unperm_shift(z.reshape(bs, S_PAD, -1)[:, ts - 1], d))
        new_s[4].append(u_full.reshape(bs, S_PAD, d_ff)[:, ts - 2:ts])

    stack = lambda u: jnp.stack(u, axis=0)
    return (yp.reshape(bp, tp, d), ys.reshape(bs, S_PAD, d)[:, :ts],
            stack(new_p[0]), stack(new_p[1]), stack(new_p[2]), stack(new_p[3]), stack(new_p[4]),
            stack(new_s[0]), stack(new_s[1]), stack(new_s[2]), stack(new_s[3]), stack(new_s[4]))
```
